```python
import jax, jax.numpy as jnp
from jax import lax
import numpy as np

D_MODEL = 4096
BATCH = 4
SEQ = 2048
DEPTH = 4
DEC_BATCH = 128
DEC_SEQ = 1
PAST_LEN = 16384
PAGE_SIZE = 128

PLE_DIM = 256
RET_HEADS = 6
RET_DK = 128
RET_DV = 256
RET_W = RET_HEADS * RET_DV
RET_CHUNK = 128
ROPE_BASE = 10000.0
HG_HEADS = 12
HG_DK = 128
HG_DV = 128
HG_W = HG_HEADS * HG_DV
HG_CHUNK = 64
CONV_W = D_MODEL - RET_W - HG_W
CONV_K = 31
MIX_W = RET_W + HG_W + CONV_W
EPS = 1e-6
SPLIT_SIZES = (RET_HEADS * RET_DK, RET_HEADS * RET_DK, RET_W, RET_W,
               HG_HEADS * HG_DK, HG_HEADS * HG_DK, HG_W, HG_W,
               CONV_W, CONV_W, CONV_W)
IN_W = sum(SPLIT_SIZES)

kernel_name = "hymba_retention_hgrn2_conformer_step"


def _split_points():
    return [int(v) for v in np.cumsum(np.array(SPLIT_SIZES))[:-1]]


def _chunk_len(T, c):
    return c if T % c == 0 else T


def _rmsnorm(x, g):
    x32 = x.astype(jnp.float32)
    y = x32 * lax.rsqrt(jnp.mean(x32 * x32, axis=-1, keepdims=True) + EPS)
    return (y * g.astype(jnp.float32)).astype(x.dtype)


def _rope(x, pos):
    half = x.shape[-1] // 2
    inv = ROPE_BASE ** (-jnp.arange(half, dtype=jnp.float32) / half)
    ang = pos[:, None] * inv[None, :]
    cos = jnp.cos(ang)[None, :, None, :]
    sin = jnp.sin(ang)[None, :, None, :]
    x1, x2 = x[..., :half], x[..., half:]
    return jnp.concatenate([x1 * cos - x2 * sin, x1 * sin + x2 * cos], axis=-1)


def _to_chunks(a, C):
    B, T, H, d = a.shape
    return a.reshape(B, T // C, C, H, d).transpose(1, 0, 3, 2, 4)


def _from_chunks(o):
    N, B, H, C, d = o.shape
    return o.transpose(1, 0, 3, 2, 4).reshape(B, N * C, H, d)


def _retention(q, k, v, s0, log_gamma):
    T = q.shape[1]
    C = _chunk_len(T, RET_CHUNK)
    idx = jnp.arange(C, dtype=jnp.float32)
    diff = idx[:, None] - idx[None, :]
    causal = diff >= 0
    lg = log_gamma[:, None, None]
    decay_in = jnp.where(causal[None], jnp.exp(jnp.where(causal, diff, 0.0)[None] * lg), 0.0)
    q_decay = jnp.exp((idx + 1.0)[None, :] * log_gamma[:, None])[..., None]
    k_decay = jnp.exp((C - 1.0 - idx)[None, :] * log_gamma[:, None])[..., None]
    chunk_decay = jnp.exp(C * log_gamma)[:, None, None]

    def step(s, inp):
        qi, ki, vi = inp
        scores = jnp.einsum('bhid,bhjd->bhij', qi, ki) * decay_in
        o = (jnp.einsum('bhij,bhjv->bhiv', scores, vi)
             + jnp.einsum('bhid,bhdv->bhiv', qi * q_decay, s))
        s = s * chunk_decay + jnp.einsum('bhjd,bhjv->bhdv', ki * k_decay, vi)
        return s, o

    s, o = lax.scan(step, s0, (_to_chunks(q, C), _to_chunks(k, C), _to_chunks(v, C)))
    return _from_chunks(o), s


def _hgrn2(q, k, logf, v, s0):
    T = q.shape[1]
    C = _chunk_len(T, HG_CHUNK)
    causal = (jnp.arange(C)[:, None] >= jnp.arange(C)[None, :])[None, None, :, :, None]

    def step(s, inp):
        qi, ki, gi, vi = inp
        b = jnp.cumsum(gi, axis=2)
        diff = b[:, :, :, None, :] - b[:, :, None, :, :]
        dec = jnp.exp(jnp.where(causal, diff, -jnp.inf))
        a = jnp.einsum('bhid,bhjd,bhijd->bhij', qi, ki, dec)
        o = (jnp.einsum('bhij,bhjv->bhiv', a, vi)
             + jnp.einsum('bhid,bhdv->bhiv', qi * jnp.exp(b), s))
        b_last = b[:, :, -1:, :]
        s = (s * jnp.exp(b_last)[:, :, 0, :, None]
             + jnp.einsum('bhjd,bhjv->bhdv', ki * jnp.exp(b_last - b), vi))
        return s, o

    s, o = lax.scan(step, s0, (_to_chunks(q, C), _to_chunks(k, C), _to_chunks(logf, C), _to_chunks(v, C)))
    return _from_chunks(o), s


def _conv_branch(a, b, buf, w_dw, b_dw, ln_g, ln_b, w_pw, b_pw):
    u = a * jax.nn.sigmoid(b)
    full = jnp.concatenate([buf.astype(jnp.float32), u], axis=1)
    new_buf = full[:, -(CONV_K - 1):]
    y = lax.conv_general_dilated(full, w_dw.astype(jnp.float32)[:, None, :], window_strides=(1,),
                                 padding='VALID', dimension_numbers=('NWC', 'WIO', 'NWC'),
                                 feature_group_count=CONV_W) + b_dw.astype(jnp.float32)
    mu = jnp.mean(y, axis=-1, keepdims=True)
    var = jnp.mean(jnp.square(y - mu), axis=-1, keepdims=True)
    y = (y - mu) * lax.rsqrt(var + EPS) * ln_g.astype(jnp.float32) + ln_b.astype(jnp.float32)
    y = jax.nn.silu(y)
    y = jnp.einsum('btc,ce->bte', y, w_pw.astype(jnp.float32)) + b_pw.astype(jnp.float32)
    return y, new_buf


def _layer(h, p_l, pos, s_ret, s_hg, s_conv, lb, log_gamma, norm_g, w_in, ret_gn_g, hg_norm_g,
           w_dw, b_dw, conv_ln_g, conv_ln_b, w_pw, b_pw, w_out, w_ple, w_pg):
    f32 = jnp.float32
    B, T, _ = h.shape
    xn = _rmsnorm(h, norm_g)
    z = jnp.einsum('btd,dc->btc', xn, w_in)
    q_r, k_r, v_r, g_r, q_h, f_h, i_h, g_h, a_c, b_c, g_c = jnp.split(z, _split_points(), axis=-1)

    q = _rope(q_r.astype(f32).reshape(B, T, RET_HEADS, RET_DK), pos)
    k = _rope(k_r.astype(f32).reshape(B, T, RET_HEADS, RET_DK), pos) * (RET_DK ** -0.5)
    v = v_r.astype(f32).reshape(B, T, RET_HEADS, RET_DV)
    o_r, s_ret_new = _retention(q, k, v, s_ret.astype(f32), log_gamma)
    mu = jnp.mean(o_r, axis=-1, keepdims=True)
    var = jnp.mean(jnp.square(o_r - mu), axis=-1, keepdims=True)
    o_r = ((o_r - mu) * lax.rsqrt(var + EPS)).reshape(B, T, RET_W) * ret_gn_g.astype(f32)
    o_r = o_r * jax.nn.silu(g_r.astype(f32))

    qh = jax.nn.silu(q_h.astype(f32)).reshape(B, T, HG_HEADS, HG_DK)
    f = lb + (1.0 - lb) * jax.nn.sigmoid(f_h.astype(f32))
    logf = jnp.log(f).reshape(B, T, HG_HEADS, HG_DK)
    kh = (1.0 - f).reshape(B, T, HG_HEADS, HG_DK)
    ih = i_h.astype(f32).reshape(B, T, HG_HEADS, HG_DV)
    o_h, s_hg_new = _hgrn2(qh, kh, logf, ih, s_hg.astype(f32))
    o_h = o_h * lax.rsqrt(jnp.mean(o_h * o_h, axis=-1, keepdims=True) + EPS)
    o_h = o_h.reshape(B, T, HG_W) * hg_norm_g.astype(f32) * jax.nn.silu(g_h.astype(f32))

    o_c, conv_new = _conv_branch(a_c.astype(f32), b_c.astype(f32), s_conv, w_dw, b_dw,
                                 conv_ln_g, conv_ln_b, w_pw, b_pw)
    o_c = o_c * jax.nn.silu(g_c.astype(f32))

    mix = jnp.concatenate([o_r, o_h, o_c], axis=-1).astype(h.dtype)
    h = h + jnp.einsum('btm,md->btd', mix, w_out)

    gate = jax.nn.sigmoid(jnp.einsum('btd,de->bte', h, w_pg).astype(f32))
    ple = jnp.einsum('btp,pd->btd', p_l, w_ple).astype(f32)
    h = h + (gate * ple).astype(h.dtype)
    return h, s_ret_new, s_hg_new, conv_new


def _trunk(x, p, pos, s_ret, s_hg, s_conv, norm_g, w_in, ret_gn_g, hg_lower_bounds, hg_norm_g,
           w_dw, b_dw, conv_ln_g, conv_ln_b, w_pw, b_pw, w_out, w_ple, w_pg, final_norm_g):
    lbs = jax.nn.softmax(hg_lower_bounds.astype(jnp.float32), axis=0)
    lbs = jnp.cumsum(lbs, axis=0) - lbs[0:1]
    log_gamma = jnp.log(1.0 - 2.0 ** (-5.0 - jnp.arange(RET_HEADS, dtype=jnp.float32)))
    h = x
    rets, hgs, convs = [], [], []
    for l in range(DEPTH):
        h, sr, sh, sc = _layer(h, p[l], pos, s_ret[l], s_hg[l], s_conv[l], lbs[l], log_gamma,
                               norm_g[l], w_in[l], ret_gn_g[l], hg_norm_g[l], w_dw[l], b_dw[l],
                               conv_ln_g[l], conv_ln_b[l], w_pw[l], b_pw[l], w_out[l], w_ple[l], w_pg[l])
        rets.append(sr.astype(x.dtype))
        hgs.append(sh.astype(x.dtype))
        convs.append(sc.astype(x.dtype))
    return _rmsnorm(h, final_norm_g), jnp.stack(rets), jnp.stack(hgs), jnp.stack(convs)


def setup_inputs(seed: int = 0) -> dict:
    key = jax.random.key(seed)
    ks = jax.random.split(key, 24)
    f32 = jnp.float32
    nrm = lambda k, s, sc: jax.random.normal(k, s, f32) * sc
    return {
        "x_prompt": nrm(ks[0], (BATCH, SEQ, D_MODEL), 1.0),
        "x_sample": nrm(ks[1], (DEC_BATCH, DEC_SEQ, D_MODEL), 1.0),
        "state_ret": nrm(ks[2], (DEPTH, DEC_BATCH, RET_HEADS, RET_DK, RET_DV), 0.1),
        "state_hgrn": nrm(ks[3], (DEPTH, DEC_BATCH, HG_HEADS, HG_DK, HG_DV), 0.5),
        "state_conv": nrm(ks[4], (DEPTH, DEC_BATCH, CONV_K - 1, CONV_W), 0.5),
        "p_prompt": nrm(ks[5], (DEPTH, BATCH, SEQ, PLE_DIM), 1.0),
        "p_sample": nrm(ks[6], (DEPTH, DEC_BATCH, DEC_SEQ, PLE_DIM), 1.0),
        "norm_g": 1.0 + nrm(ks[7], (DEPTH, D_MODEL), 0.01),
        "w_in": nrm(ks[8], (DEPTH, D_MODEL, IN_W), D_MODEL ** -0.5),
        "ret_gn_g": 1.0 + nrm(ks[9], (DEPTH, RET_W), 0.01),
        "hg_lower_bounds": nrm(ks[10], (DEPTH, HG_HEADS * HG_DK), 1.0),
        "hg_norm_g": 1.0 + nrm(ks[11], (DEPTH, HG_W), 0.01),
        "w_dw": nrm(ks[12], (DEPTH, CONV_K, CONV_W), CONV_K ** -0.5),
        "b_dw": nrm(ks[13], (DEPTH, CONV_W), 0.01),
        "conv_ln_g": 1.0 + nrm(ks[14], (DEPTH, CONV_W), 0.01),
        "conv_ln_b": nrm(ks[15], (DEPTH, CONV_W), 0.01),
        "w_pw": nrm(ks[16], (DEPTH, CONV_W, CONV_W), CONV_W ** -0.5),
        "b_pw": nrm(ks[17], (DEPTH, CONV_W), 0.01),
        "w_out": nrm(ks[18], (DEPTH, MIX_W, D_MODEL), MIX_W ** -0.5),
        "w_ple": nrm(ks[19], (DEPTH, PLE_DIM, D_MODEL), PLE_DIM ** -0.5),
        "w_pg": nrm(ks[20], (DEPTH, D_MODEL, D_MODEL), D_MODEL ** -0.5),
        "final_norm_g": 1.0 + nrm(ks[21], (D_MODEL,), 0.01),
    }


def reference(x_prompt, x_sample, state_ret, state_hgrn, state_conv, p_prompt, p_sample,
              norm_g, w_in, ret_gn_g, hg_lower_bounds, hg_norm_g, w_dw, b_dw, conv_ln_g, conv_ln_b,
              w_pw, b_pw, w_out, w_ple, w_pg, final_norm_g):
    weights = (norm_g, w_in, ret_gn_g, hg_lower_bounds, hg_norm_g, w_dw, b_dw, conv_ln_g, conv_ln_b,
               w_pw, b_pw, w_out, w_ple, w_pg, final_norm_g)
    bp, tp = x_prompt.shape[0], x_prompt.shape[1]
    ts = x_sample.shape[1]
    z_ret = jnp.zeros((DEPTH, bp, RET_HEADS, RET_DK, RET_DV), x_prompt.dtype)
    z_hg = jnp.zeros((DEPTH, bp, HG_HEADS, HG_DK, HG_DV), x_prompt.dtype)
    z_conv = jnp.zeros((DEPTH, bp, CONV_K - 1, CONV_W), x_prompt.dtype)
    pos_p = jnp.arange(tp, dtype=jnp.float32)
    y_prompt, ret_p, hg_p, conv_p = _trunk(x_prompt, p_prompt, pos_p, z_ret, z_hg, z_conv, *weights)
    pos_s = PAST_LEN + jnp.arange(ts, dtype=jnp.float32)
    y_sample, ret_s, hg_s, conv_s = _trunk(x_sample, p_sample, pos_s, state_ret, state_hgrn, state_conv, *weights)
    return (y_prompt, y_sample, ret_p, hg_p, conv_p, ret_s, hg_s, conv_s)
```

```python
import functools
import math

import numpy as np
import jax
import jax.numpy as jnp
from jax import lax
from jax.experimental import pallas as pl
from jax.experimental.pallas import tpu as pltpu

F32 = jnp.float32
BF16 = jnp.bfloat16

D_MODEL = 4096
DEPTH = 4
PAST_LEN = 16384
PLE_DIM = 256
RET_HEADS = 6
RET_DK = 128
RET_DV = 256
RET_W = RET_HEADS * RET_DV
ROPE_BASE = 10000.0
HG_HEADS = 12
HG_DK = 128
HG_DV = 128
HG_W = HG_HEADS * HG_DV
CONV_W = D_MODEL - RET_W - HG_W
CONV_K = 31
EPS = 1e-6
IN_W = 2 * RET_HEADS * RET_DK + 2 * RET_W + 2 * HG_HEADS * HG_DK + 2 * HG_W + 3 * CONV_W

OFF_QR = 0
OFF_KR = OFF_QR + RET_HEADS * RET_DK
OFF_VR = OFF_KR + RET_HEADS * RET_DK
OFF_GR = OFF_VR + RET_W
OFF_QH = OFF_GR + RET_W
OFF_FH = OFF_QH + HG_HEADS * HG_DK
OFF_IH = OFF_FH + HG_HEADS * HG_DK
OFF_GH = OFF_IH + HG_W
OFF_AC = OFF_GH + HG_W
OFF_BC = OFF_AC + CONV_W
OFF_GC = OFF_BC + CONV_W

V7X_VMEM_LIMIT_BYTES = 56 * 1024 * 1024
CHUNK = 128
HIST = 32
DEC_BB = 8

_NT = (((1,), (1,)), ((), ()))


def _params(sem):
    return pltpu.CompilerParams(dimension_semantics=sem, vmem_limit_bytes=V7X_VMEM_LIMIT_BYTES)


def _sigmoid(x):
    return 1.0 / (1.0 + jnp.exp(-x))


def _silu(x):
    return x * _sigmoid(x)


def _rmsnorm_body(x_ref, g_ref, o_ref):
    x = x_ref[...]
    ms = jnp.mean(x * x, axis=-1, keepdims=True)
    o_ref[...] = (x * lax.rsqrt(ms + EPS) * g_ref[...]).astype(o_ref.dtype)


def _rmsnorm(x, g, out_dtype, tm):
    m, d = x.shape
    return pl.pallas_call(
        _rmsnorm_body,
        grid=(m // tm,),
        in_specs=[pl.BlockSpec((tm, d), lambda i: (i, 0)),
                  pl.BlockSpec((1, d), lambda i: (0, 0))],
        out_specs=pl.BlockSpec((tm, d), lambda i: (i, 0)),
        out_shape=jax.ShapeDtypeStruct((m, d), out_dtype),
        compiler_params=_params(("parallel",)),
        name="rmsnorm",
    )(x, g.reshape(1, d))


def _mm_body(x_ref, w_ref, *rest, nk, mode):
    acc_ref = rest[-1]
    k = pl.program_id(2)
    part = jnp.dot(x_ref[...], w_ref[...], preferred_element_type=F32)

    @pl.when(k == 0)
    def _():
        acc_ref[...] = part

    @pl.when(k > 0)
    def _():
        acc_ref[...] += part

    @pl.when(k == nk - 1)
    def _():
        acc = acc_ref[...]
        if mode == "plain":
            (o_ref,) = rest[:-1]
            o_ref[...] = acc
        elif mode == "resid":
            r_ref, o_ref, ob_ref = rest[:-1]
            h1 = r_ref[...] + acc
            o_ref[...] = h1
            ob_ref[...] = h1.astype(BF16)
        else:
            r_ref, p_ref, wp_ref, o_ref = rest[:-1]
            ple = jnp.dot(p_ref[...], wp_ref[...], preferred_element_type=F32)
            o_ref[...] = r_ref[...] + _sigmoid(acc) * ple


def _mm(x, w_all, layer, *, mode, tm, tn, tk, resid=None, p=None, wp_all=None):
    m, kdim = x.shape
    n = w_all.shape[-1]
    nk = kdim // tk
    grid = (m // tm, n // tn, nk)
    in_specs = [pl.BlockSpec((tm, tk), lambda i, j, k: (i, k)),
                pl.BlockSpec((None, tk, tn), lambda i, j, k: (layer, k, j))]
    args = [x, w_all]
    o_spec = pl.BlockSpec((tm, tn), lambda i, j, k: (i, j))
    if mode == "plain":
        out_specs = o_spec
        out_shape = jax.ShapeDtypeStruct((m, n), F32)
    elif mode == "resid":
        in_specs.append(o_spec)
        args.append(resid)
        out_specs = [o_spec, o_spec]
        out_shape = [jax.ShapeDtypeStruct((m, n), F32), jax.ShapeDtypeStruct((m, n), BF16)]
    else:
        pdim = p.shape[-1]
        in_specs += [o_spec,
                     pl.BlockSpec((None, tm, pdim), lambda i, j, k: (layer, i, 0)),
                     pl.BlockSpec((None, pdim, tn), lambda i, j, k: (layer, 0, j))]
        args += [resid, p, wp_all]
        out_specs = o_spec
        out_shape = jax.ShapeDtypeStruct((m, n), F32)
    return pl.pallas_call(
        functools.partial(_mm_body, nk=nk, mode=mode),
        grid=grid,
        in_specs=in_specs,
        out_specs=out_specs,
        out_shape=out_shape,
        scratch_shapes=[pltpu.VMEM((tm, tn), F32)],
        compiler_params=_params(("parallel", "parallel", "arbitrary")),
        name="mm_" + mode,
    )(*args)


def _ret_prefill_body(q_ref, k_ref, v_ref, g_ref, cos_ref, sin_ref, lg_ref, gn_ref,
                      o_ref, s_ref, st_scr):
    c = pl.program_id(1)
    nc = pl.num_programs(1)
    L = CHUNK

    @pl.when(c == 0)
    def _():
        st_scr[...] = jnp.zeros_like(st_scr)

    cos = cos_ref[...]
    sin = sin_ref[...]
    row = lax.broadcasted_iota(jnp.int32, (L, L), 0)
    col = lax.broadcasted_iota(jnp.int32, (L, L), 1)
    diff = (row - col).astype(F32)
    causal = row >= col
    rowf = lax.broadcasted_iota(jnp.int32, (L, RET_DK), 0).astype(F32)
    for h in range(RET_HEADS):
        lg = lg_ref[h:h + 1, :]
        lg1 = lg[:, :RET_DK]
        decay_in = jnp.where(causal, jnp.exp(jnp.where(causal, diff, 0.0) * lg1), 0.0)
        q_decay = jnp.exp((rowf + 1.0) * lg1)
        k_decay = jnp.exp((L - 1.0 - rowf) * lg1)
        chunk_decay = jnp.exp(L * lg)

        zq = q_ref[:, h * RET_DK:(h + 1) * RET_DK]
        zk = k_ref[:, h * RET_DK:(h + 1) * RET_DK]
        v = v_ref[:, h * RET_DV:(h + 1) * RET_DV]
        zg = g_ref[:, h * RET_DV:(h + 1) * RET_DV]
        q = zq * cos + pltpu.roll(zq, RET_DK // 2, 1) * sin
        k = (zk * cos + pltpu.roll(zk, RET_DK // 2, 1) * sin) * (RET_DK ** -0.5)
        vb = v.astype(BF16)
        s0 = st_scr[h]

        scores = lax.dot_general(q.astype(BF16), k.astype(BF16), _NT,
                                 preferred_element_type=F32) * decay_in
        o = (jnp.dot(scores.astype(BF16), vb, preferred_element_type=F32)
             + jnp.dot((q * q_decay).astype(BF16), s0.astype(BF16), preferred_element_type=F32))
        kT = (k * k_decay).T.astype(BF16)
        s1 = s0 * chunk_decay + jnp.dot(kT, vb, preferred_element_type=F32)
        st_scr[h] = s1

        mu = jnp.mean(o, axis=-1, keepdims=True)
        var = jnp.mean(jnp.square(o - mu), axis=-1, keepdims=True)
        on = (o - mu) * lax.rsqrt(var + EPS) * gn_ref[:, h * RET_DV:(h + 1) * RET_DV]
        o_ref[:, h * RET_DV:(h + 1) * RET_DV] = (on * _silu(zg)).astype(o_ref.dtype)

        @pl.when(c == nc - 1)
        def _():
            s_ref[0, h] = s1


def _ret_prefill(z, cos_t, sin_t, lg_t, gn_g, batch, seq):
    m = z.shape[0]
    nc = seq // CHUNK
    L = CHUNK
    rowmap = lambda blk: (lambda b, c: (b * nc + c, blk))
    return pl.pallas_call(
        _ret_prefill_body,
        grid=(batch, nc),
        in_specs=[pl.BlockSpec((L, RET_HEADS * RET_DK), rowmap(OFF_QR // (RET_HEADS * RET_DK))),
                  pl.BlockSpec((L, RET_HEADS * RET_DK), rowmap(OFF_KR // (RET_HEADS * RET_DK))),
                  pl.BlockSpec((L, RET_W), rowmap(OFF_VR // RET_W)),
                  pl.BlockSpec((L, RET_W), rowmap(OFF_GR // RET_W)),
                  pl.BlockSpec((L, RET_DK), lambda b, c: (c, 0)),
                  pl.BlockSpec((L, RET_DK), lambda b, c: (c, 0)),
                  pl.BlockSpec((RET_HEADS, RET_DV), lambda b, c: (0, 0)),
                  pl.BlockSpec((1, RET_W), lambda b, c: (0, 0))],
        out_specs=[pl.BlockSpec((L, RET_W), lambda b, c: (b * nc + c, 0)),
                   pl.BlockSpec((1, RET_HEADS, RET_DK, RET_DV), lambda b, c: (b, 0, 0, 0))],
        out_shape=[jax.ShapeDtypeStruct((m, RET_W), BF16),
                   jax.ShapeDtypeStruct((batch, RET_HEADS, RET_DK, RET_DV), F32)],
        scratch_shapes=[pltpu.VMEM((RET_HEADS, RET_DK, RET_DV), F32)],
        compiler_params=_params(("parallel", "arbitrary")),
        name="ret_prefill",
    )(z, z, z, z, cos_t, sin_t, lg_t, gn_g.reshape(1, RET_W))


HG_HB = 6
_NLEV = int(math.log2(CHUNK))


def _level_table():
    i = np.arange(CHUNK)[:, None]
    j = np.arange(CHUNK)[None, :]
    x = np.bitwise_xor(i, j)
    lvl = np.floor(np.log2(np.maximum(x, 1))).astype(np.int32)
    lvl = np.where(i == j, _NLEV, lvl)
    lvl = np.where(i < j, -1, lvl)
    return lvl.astype(np.int32)


def _ref_rows(b, m):
    L, w = b.shape
    g = 2 * m
    b3 = b.reshape(L // g, g, w)
    r = jnp.broadcast_to(b3[:, m - 1:m, :], b3.shape)
    return r.reshape(L, w)


def _hgrn_prefill_body(q_ref, f_ref, i_ref, g_ref, lb_ref, ng_ref, tri_ref, lvl_ref,
                       o_ref, s_ref, st_scr):
    c = pl.program_id(2)
    nc = pl.num_programs(2)
    L = CHUNK

    @pl.when(c == 0)
    def _():
        st_scr[...] = jnp.zeros_like(st_scr)

    tri = tri_ref[...]
    lvl = lvl_ref[...]
    rowi = lax.broadcasted_iota(jnp.int32, (L, HG_DK), 0)
    for hh in range(HG_HB):
        sl = slice(hh * HG_DK, (hh + 1) * HG_DK)
        zq = q_ref[:, sl]
        zf = f_ref[:, sl]
        v = i_ref[:, sl]
        zg = g_ref[:, sl]
        lb = lb_ref[:, sl]
        q = _silu(zq)
        f = lb + (1.0 - lb) * _sigmoid(zf)
        logf = jnp.log(f)
        k = 1.0 - f
        b = jnp.dot(tri, logf, precision=lax.Precision.HIGHEST, preferred_element_type=F32)
        qb = q.astype(BF16)
        kb = k.astype(BF16)
        vb = v.astype(BF16)

        a = jnp.where(lvl == _NLEV, lax.dot_general(qb, kb, _NT, preferred_element_type=F32), 0.0)
        f_prev = pltpu.roll(f, 1, 0)
        f_next = pltpu.roll(f, L - 1, 0)
        for t in range(_NLEV):
            m = 1 << t
            if m == 1:
                e = jnp.where((rowi & 1) == 1, f, 1.0)
            elif m == 2:
                r4 = rowi & 3
                e = jnp.where(r4 == 0, f_next, jnp.where(r4 == 1, 1.0, jnp.where(r4 == 2, f, f * f_prev)))
            else:
                e = jnp.exp(-jnp.abs(b - _ref_rows(b, m)))
            p = lax.dot_general((q * e).astype(BF16), (k * e).astype(BF16), _NT,
                                preferred_element_type=F32)
            a = a + jnp.where(lvl == t, p, 0.0)

        st = st_scr[hh]
        o = (jnp.dot(a.astype(BF16), vb, preferred_element_type=F32)
             + lax.dot_general((q * jnp.exp(b)).astype(BF16), st.astype(BF16), _NT,
                               preferred_element_type=F32))
        b_last = b[L - 1:L, :]
        k_s = (k * jnp.exp(b_last - b)).astype(BF16)
        st1 = st * jnp.exp(b_last) + jnp.dot(v.T.astype(BF16), k_s, preferred_element_type=F32)
        st_scr[hh] = st1

        on = o * lax.rsqrt(jnp.mean(o * o, axis=-1, keepdims=True) + EPS) * ng_ref[:, sl]
        o_ref[:, sl] = (on * _silu(zg)).astype(o_ref.dtype)

        @pl.when(c == nc - 1)
        def _():
            s_ref[0, hh] = st1.T


def _hgrn_prefill(z, lb, ng, batch, seq):
    m = z.shape[0]
    nc = seq // CHUNK
    L = CHUNK
    ng_groups = HG_HEADS // HG_HB
    wblk = HG_HB * HG_DK
    rowmap = lambda off: (lambda b, g, c: (b * nc + c, off // wblk + g))
    tri = jnp.asarray(np.tril(np.ones((L, L), np.float32)))
    lvl = jnp.asarray(_level_table())
    return pl.pallas_call(
        _hgrn_prefill_body,
        grid=(batch, ng_groups, nc),
        in_specs=[pl.BlockSpec((L, wblk), rowmap(OFF_QH)),
                  pl.BlockSpec((L, wblk), rowmap(OFF_FH)),
                  pl.BlockSpec((L, wblk), rowmap(OFF_IH)),
                  pl.BlockSpec((L, wblk), rowmap(OFF_GH)),
                  pl.BlockSpec((1, wblk), lambda b, g, c: (0, g)),
                  pl.BlockSpec((1, wblk), lambda b, g, c: (0, g)),
                  pl.BlockSpec((L, L), lambda b, g, c: (0, 0)),
                  pl.BlockSpec((L, L), lambda b, g, c: (0, 0))],
        out_specs=[pl.BlockSpec((L, wblk), lambda b, g, c: (b * nc + c, g)),
                   pl.BlockSpec((1, HG_HB, HG_DK, HG_DV), lambda b, g, c: (b, g, 0, 0))],
        out_shape=[jax.ShapeDtypeStruct((m, HG_W), BF16),
                   jax.ShapeDtypeStruct((batch, HG_HEADS, HG_DK, HG_DV), F32)],
        scratch_shapes=[pltpu.VMEM((HG_HB, HG_DV, HG_DK), F32)],
        compiler_params=_params(("parallel", "parallel", "arbitrary")),
        name="hgrn_prefill",
    )(z, z, z, z, lb.reshape(1, HG_W), ng.reshape(1, HG_W), tri, lvl)


CONV_TT = 256
CONV_RC = 32
_CBLK = 1536


def _conv_tail(y, g, ln_g_ref, ln_b_ref, wpw_ref, bpw_ref):
    mu = jnp.mean(y, axis=-1, keepdims=True)
    var = jnp.mean(jnp.square(y - mu), axis=-1, keepdims=True)
    yn = (y - mu) * lax.rsqrt(var + EPS) * ln_g_ref[...] + ln_b_ref[...]
    pw = jnp.dot(_silu(yn).astype(BF16), wpw_ref[...], preferred_element_type=F32) + bpw_ref[...]
    return pw * _silu(g)


def _conv_prefill_body(z7_ref, z8_ref, wdw_ref, bdw_ref, ln_g_ref, ln_b_ref, wpw_ref, bpw_ref,
                       o_ref, s_ref, ext, ybuf):
    t = pl.program_id(1)
    nt = pl.num_programs(1)
    tt = CONV_TT

    @pl.when(t == 0)
    def _():
        ext[0:HIST, :] = jnp.zeros((HIST, CONV_W), F32)

    a = z7_ref[:, 0:CONV_W]
    bgate = jnp.concatenate([z7_ref[:, CONV_W:_CBLK], z8_ref[:, 0:2 * CONV_W - _CBLK]], axis=-1)
    g = z8_ref[:, 2 * CONV_W - _CBLK:]
    ext[HIST:HIST + tt, :] = a * _sigmoid(bgate)

    base = HIST - (CONV_K - 1)

    for ci in range(tt // CONV_RC):
        r0 = ci * CONV_RC
        acc = jnp.zeros((CONV_RC, CONV_W), F32)
        for k in range(CONV_K):
            acc = acc + wdw_ref[k:k + 1, :] * ext[r0 + base + k:r0 + base + k + CONV_RC, :]
        ybuf[r0:r0 + CONV_RC, :] = acc
    y = ybuf[...] + bdw_ref[...]
    o_ref[...] = _conv_tail(y, g, ln_g_ref, ln_b_ref, wpw_ref, bpw_ref).astype(o_ref.dtype)

    @pl.when(t == nt - 1)
    def _():
        s_ref[0] = ext[tt + HIST - (CONV_K - 1):tt + HIST, :]

    ext[0:HIST, :] = ext[tt:tt + HIST, :]


def _conv_prefill(z, w_dw, b_dw, ln_g, ln_b, wpw_all, layer, b_pw, batch, seq):
    m = z.shape[0]
    tt = CONV_TT
    nt = seq // tt
    vec = lambda i, t: (0, 0)
    return pl.pallas_call(
        _conv_prefill_body,
        grid=(batch, nt),
        in_specs=[pl.BlockSpec((tt, _CBLK), lambda b, t: (b * nt + t, OFF_AC // _CBLK)),
                  pl.BlockSpec((tt, _CBLK), lambda b, t: (b * nt + t, OFF_AC // _CBLK + 1)),
                  pl.BlockSpec((CONV_K, CONV_W), vec),
                  pl.BlockSpec((1, CONV_W), vec),
                  pl.BlockSpec((1, CONV_W), vec),
                  pl.BlockSpec((1, CONV_W), vec),
                  pl.BlockSpec((None, CONV_W, CONV_W), lambda b, t: (layer, 0, 0)),
                  pl.BlockSpec((1, CONV_W), vec)],
        out_specs=[pl.BlockSpec((tt, CONV_W), lambda b, t: (b * nt + t, 0)),
                   pl.BlockSpec((1, CONV_K - 1, CONV_W), lambda b, t: (b, 0, 0))],
        out_shape=[jax.ShapeDtypeStruct((m, CONV_W), BF16),
                   jax.ShapeDtypeStruct((batch, CONV_K - 1, CONV_W), F32)],
        scratch_shapes=[pltpu.VMEM((HIST + tt, CONV_W), F32), pltpu.VMEM((tt, CONV_W), F32)],
        compiler_params=_params(("parallel", "arbitrary")),
        name="conv_prefill",
    )(z, z, w_dw, b_dw.reshape(1, -1), ln_g.reshape(1, -1), ln_b.reshape(1, -1), wpw_all,
      b_pw.reshape(1, -1))


def _ret_decode_body(qT_ref, kT_ref, v_ref, g_ref, cos_ref, sin_ref, lg_ref, gn_ref, s_ref,
                     o_ref, so_ref, orow):
    half = RET_DK // 2
    cosc = cos_ref[...]
    sinc = sin_ref[...]
    for h in range(RET_HEADS):
        zq = qT_ref[h]
        zk = kT_ref[h]
        q = zq * cosc + jnp.concatenate([zq[half:], zq[:half]], axis=0) * sinc
        k = (zk * cosc + jnp.concatenate([zk[half:], zk[:half]], axis=0) * sinc) * (RET_DK ** -0.5)
        gamma = jnp.exp(lg_ref[h:h + 1, :])
        for j in range(DEC_BB):
            vrow = v_ref[j:j + 1, h * RET_DV:(h + 1) * RET_DV]
            s1 = gamma * s_ref[j, h] + k[:, j:j + 1] * vrow
            so_ref[j, h] = s1
            orow[j:j + 1, h * RET_DV:(h + 1) * RET_DV] = jnp.sum(q[:, j:j + 1] * s1, axis=0, keepdims=True)
    for h in range(RET_HEADS):
        sl = slice(h * RET_DV, (h + 1) * RET_DV)
        o = orow[:, sl]
        mu = jnp.mean(o, axis=-1, keepdims=True)
        var = jnp.mean(jnp.square(o - mu), axis=-1, keepdims=True)
        on = (o - mu) * lax.rsqrt(var + EPS) * gn_ref[:, sl]
        o_ref[:, sl] = on * _silu(g_ref[:, sl])


def _cols(zs, off, heads, dk):
    bsz = zs.shape[0]
    x = zs[:, off:off + heads * dk].reshape(bsz // DEC_BB, DEC_BB, heads, dk)
    return x.transpose(0, 2, 3, 1)


def _ret_decode(zs, state_all, layer, cosc, sinc, lg_t, gn_g):
    bsz = zs.shape[0]
    bb = DEC_BB
    qT = _cols(zs, OFF_QR, RET_HEADS, RET_DK)
    kT = _cols(zs, OFF_KR, RET_HEADS, RET_DK)
    col_spec = pl.BlockSpec((None, RET_HEADS, RET_DK, bb), lambda i: (i, 0, 0, 0))
    return pl.pallas_call(
        _ret_decode_body,
        grid=(bsz // bb,),
        in_specs=[col_spec, col_spec,
                  pl.BlockSpec((bb, RET_W), lambda i: (i, OFF_VR // RET_W)),
                  pl.BlockSpec((bb, RET_W), lambda i: (i, OFF_GR // RET_W)),
                  pl.BlockSpec((RET_DK, bb), lambda i: (0, 0)),
                  pl.BlockSpec((RET_DK, bb), lambda i: (0, 0)),
                  pl.BlockSpec((RET_HEADS, RET_DV), lambda i: (0, 0)),
                  pl.BlockSpec((1, RET_W), lambda i: (0, 0)),
                  pl.BlockSpec((None, bb, RET_HEADS, RET_DK, RET_DV), lambda i: (layer, i, 0, 0, 0))],
        out_specs=[pl.BlockSpec((bb, RET_W), lambda i: (i, 0)),
                   pl.BlockSpec((bb, RET_HEADS, RET_DK, RET_DV), lambda i: (i, 0, 0, 0))],
        out_shape=[jax.ShapeDtypeStruct((bsz, RET_W), F32),
                   jax.ShapeDtypeStruct((bsz, RET_HEADS, RET_DK, RET_DV), F32)],
        scratch_shapes=[pltpu.VMEM((bb, RET_W), F32)],
        compiler_params=_params(("parallel",)),
        name="ret_decode",
    )(qT, kT, zs, zs, cosc, sinc, lg_t, gn_g.reshape(1, RET_W), state_all)


def _hgrn_decode_body(qT_ref, fT_ref, v_ref, g_ref, lbT_ref, ng_ref, s_ref, o_ref, so_ref, orow):
    for h in range(HG_HEADS):
        q = _silu(qT_ref[h])
        lb = lbT_ref[h]
        f = lb + (1.0 - lb) * _sigmoid(fT_ref[h])
        k = 1.0 - f
        for j in range(DEC_BB):
            vrow = v_ref[j:j + 1, h * HG_DV:(h + 1) * HG_DV]
            s1 = f[:, j:j + 1] * s_ref[j, h] + k[:, j:j + 1] * vrow
            so_ref[j, h] = s1
            orow[j:j + 1, h * HG_DV:(h + 1) * HG_DV] = jnp.sum(q[:, j:j + 1] * s1, axis=0, keepdims=True)
    for h in range(HG_HEADS):
        sl = slice(h * HG_DV, (h + 1) * HG_DV)
        o = orow[:, sl]
        on = o * lax.rsqrt(jnp.mean(o * o, axis=-1, keepdims=True) + EPS) * ng_ref[:, sl]
        o_ref[:, sl] = on * _silu(g_ref[:, sl])


def _hgrn_decode(zs, state_all, layer, lb, ng):
    bsz = zs.shape[0]
    bb = DEC_BB
    qT = _cols(zs, OFF_QH, HG_HEADS, HG_DK)
    fT = _cols(zs, OFF_FH, HG_HEADS, HG_DK)
    col_spec = pl.BlockSpec((None, HG_HEADS, HG_DK, bb), lambda i: (i, 0, 0, 0))
    return pl.pallas_call(
        _hgrn_decode_body,
        grid=(bsz // bb,),
        in_specs=[col_spec, col_spec,
                  pl.BlockSpec((bb, HG_W), lambda i: (i, OFF_IH // HG_W)),
                  pl.BlockSpec((bb, HG_W), lambda i: (i, OFF_GH // HG_W)),
                  pl.BlockSpec((HG_HEADS, HG_DK, 1), lambda i: (0, 0, 0)),
                  pl.BlockSpec((1, HG_W), lambda i: (0, 0)),
                  pl.BlockSpec((None, bb, HG_HEADS, HG_DK, HG_DV), lambda i: (layer, i, 0, 0, 0))],
        out_specs=[pl.BlockSpec((bb, HG_W), lambda i: (i, 0)),
                   pl.BlockSpec((bb, HG_HEADS, HG_DK, HG_DV), lambda i: (i, 0, 0, 0))],
        out_shape=[jax.ShapeDtypeStruct((bsz, HG_W), F32),
                   jax.ShapeDtypeStruct((bsz, HG_HEADS, HG_DK, HG_DV), F32)],
        scratch_shapes=[pltpu.VMEM((bb, HG_W), F32)],
        compiler_params=_params(("parallel",)),
        name="hgrn_decode",
    )(qT, fT, zs, zs, lb.reshape(HG_HEADS, HG_DK, 1), ng.reshape(1, HG_W), state_all)


def _conv_decode_body(z7_ref, z8_ref, wdw_ref, bdw_ref, ln_g_ref, ln_b_ref, wpw_ref, bpw_ref, s_ref,
                      o_ref, so_ref):
    kh = CONV_K - 1
    a = z7_ref[:, 0:CONV_W]
    bgate = jnp.concatenate([z7_ref[:, CONV_W:_CBLK], z8_ref[:, 0:2 * CONV_W - _CBLK]], axis=-1)
    g = z8_ref[:, 2 * CONV_W - _CBLK:]
    u = a * _sigmoid(bgate)
    buf = s_ref[...]
    y = jnp.sum(buf * wdw_ref[0:kh, :][None], axis=1) + u * wdw_ref[kh:kh + 1, :] + bdw_ref[...]
    o_ref[...] = _conv_tail(y, g, ln_g_ref, ln_b_ref, wpw_ref, bpw_ref)
    so_ref[:, 0:kh - 1, :] = s_ref[:, 1:kh, :]
    for j in range(DEC_BB):
        so_ref[j, kh - 1:kh, :] = u[j:j + 1, :]


def _conv_decode(zs, state_all, layer, w_dw, b_dw, ln_g, ln_b, wpw_all, b_pw):
    bsz = zs.shape[0]
    bb = DEC_BB
    vec = lambda i: (0, 0)
    return pl.pallas_call(
        _conv_decode_body,
        grid=(bsz // bb,),
        in_specs=[pl.BlockSpec((bb, _CBLK), lambda i: (i, OFF_AC // _CBLK)),
                  pl.BlockSpec((bb, _CBLK), lambda i: (i, OFF_AC // _CBLK + 1)),
                  pl.BlockSpec((CONV_K, CONV_W), vec),
                  pl.BlockSpec((1, CONV_W), vec),
                  pl.BlockSpec((1, CONV_W), vec),
                  pl.BlockSpec((1, CONV_W), vec),
                  pl.BlockSpec((None, CONV_W, CONV_W), lambda i: (layer, 0, 0)),
                  pl.BlockSpec((1, CONV_W), vec),
                  pl.BlockSpec((None, bb, CONV_K - 1, CONV_W), lambda i: (layer, i, 0, 0))],
        out_specs=[pl.BlockSpec((bb, CONV_W), lambda i: (i, 0)),
                   pl.BlockSpec((bb, CONV_K - 1, CONV_W), lambda i: (i, 0, 0))],
        out_shape=[jax.ShapeDtypeStruct((bsz, CONV_W), F32),
                   jax.ShapeDtypeStruct((bsz, CONV_K - 1, CONV_W), F32)],
        compiler_params=_params(("parallel",)),
        name="conv_decode",
    )(zs, zs, w_dw, b_dw.reshape(1, -1), ln_g.reshape(1, -1), ln_b.reshape(1, -1), wpw_all,
      b_pw.reshape(1, -1), state_all)


def _rope_tables(pos):
    half = RET_DK // 2
    inv = ROPE_BASE ** (-jnp.arange(half, dtype=F32) / half)
    ang = pos[:, None] * inv[None, :]
    cos = jnp.cos(ang)
    sin = jnp.sin(ang)
    return jnp.concatenate([cos, cos], axis=-1), jnp.concatenate([-sin, sin], axis=-1)


def _tiles(m):
    if m >= 1024:
        return 1024, 1536, 1024, 1024
    return m, 1536, 1024, D_MODEL


def kernel(x_prompt, x_sample, state_ret, state_hgrn, state_conv, p_prompt, p_sample, norm_g, w_in,
           ret_gn_g, hg_lower_bounds, hg_norm_g, w_dw, b_dw, conv_ln_g, conv_ln_b, w_pw, b_pw, w_out,
           w_ple, w_pg, final_norm_g):
    bp, tp, d = x_prompt.shape
    bs, ts, _ = x_sample.shape
    assert ts == 1 and tp % CONV_TT == 0 and bs % DEC_BB == 0

    w_in_b = w_in.astype(BF16)
    w_out_b = w_out.astype(BF16)
    w_pg_b = w_pg.astype(BF16)
    w_ple_b = w_ple.astype(BF16)
    w_pw_b = w_pw.astype(BF16)

    lbs = jax.nn.softmax(hg_lower_bounds.astype(F32), axis=0)
    lbs = jnp.cumsum(lbs, axis=0) - lbs[0:1]
    log_gamma = np.log(1.0 - 2.0 ** (-5.0 - np.arange(RET_HEADS, dtype=np.float32))).astype(np.float32)
    lg_t = jnp.asarray(np.broadcast_to(log_gamma[:, None], (RET_HEADS, RET_DV)))

    cos_p, sin_p = _rope_tables(jnp.arange(tp, dtype=F32))
    cos_s, sin_s = _rope_tables(PAST_LEN + jnp.arange(ts, dtype=F32))
    cosc = jnp.broadcast_to(cos_s.reshape(RET_DK, 1), (RET_DK, DEC_BB))
    sinc = jnp.broadcast_to(sin_s.reshape(RET_DK, 1), (RET_DK, DEC_BB))

    def layer(h, l, p_b, prompt):
        m = h.shape[0]
        tm, tn_in, tn_sq, tk = _tiles(m)
        xn = _rmsnorm(h, norm_g[l], BF16, min(m, 256))
        z = _mm(xn, w_in_b, l, mode="plain", tm=tm, tn=tn_in, tk=tk)
        if prompt:
            mix_r, s_r = _ret_prefill(z, cos_p, sin_p, lg_t, ret_gn_g[l], bp, tp)
            mix_h, s_h = _hgrn_prefill(z, lbs[l], hg_norm_g[l], bp, tp)
            mix_c, s_c = _conv_prefill(z, w_dw[l], b_dw[l], conv_ln_g[l], conv_ln_b[l], w_pw_b, l,
                                       b_pw[l], bp, tp)
        else:
            mix_r, s_r = _ret_decode(z, state_ret, l, cosc, sinc, lg_t, ret_gn_g[l])
            mix_h, s_h = _hgrn_decode(z, state_hgrn, l, lbs[l], hg_norm_g[l])
            mix_c, s_c = _conv_decode(z, state_conv, l, w_dw[l], b_dw[l], conv_ln_g[l], conv_ln_b[l],
                                      w_pw_b, b_pw[l])
        mix = jnp.concatenate([mix_r, mix_h, mix_c], axis=-1).astype(BF16)
        h1, h1b = _mm(mix, w_out_b, l, mode="resid", tm=tm, tn=tn_sq, tk=tk, resid=h)
        h2 = _mm(h1b, w_pg_b, l, mode="ple", tm=tm, tn=tn_sq, tk=tk, resid=h1, p=p_b, wp_all=w_ple_b)
        return h2, s_r, s_h, s_c

    def trunk(x, p, prompt):
        m = x.shape[0] * x.shape[1]
        h = x.reshape(m, d)
        p_b = p.reshape(DEPTH, m, PLE_DIM).astype(BF16)
        rets, hgs, convs = [], [], []
        for l in range(DEPTH):
            h, s_r, s_h, s_c = layer(h, l, p_b, prompt)
            rets.append(s_r)
            hgs.append(s_h)
            convs.append(s_c)
        y = _rmsnorm(h, final_norm_g, F32, min(m, 256)).reshape(x.shape)
        return y, jnp.stack(rets), jnp.stack(hgs), jnp.stack(convs)

    y_p, ret_p, hg_p, conv_p = trunk(x_prompt, p_prompt, True)
    y_s, ret_s, hg_s, conv_s = trunk(x_sample, p_sample, False)
    return (y_p, y_s, ret_p, hg_p, conv_p, ret_s, hg_s, conv_s)
```

```python
import functools
import math

import numpy as np
import jax
import jax.numpy as jnp
from jax import lax
from jax.experimental import pallas as pl
from jax.experimental.pallas import tpu as pltpu

F32 = jnp.float32
BF16 = jnp.bfloat16

D_MODEL = 4096
DEPTH = 4
PAST_LEN = 16384
PLE_DIM = 256
RET_HEADS = 6
RET_DK = 128
RET_DV = 256
RET_W = RET_HEADS * RET_DV
ROPE_BASE = 10000.0
HG_HEADS = 12
HG_DK = 128
HG_DV = 128
HG_W = HG_HEADS * HG_DV
CONV_W = D_MODEL - RET_W - HG_W
CONV_K = 31
EPS = 1e-6
IN_W = 2 * RET_HEADS * RET_DK + 2 * RET_W + 2 * HG_HEADS * HG_DK + 2 * HG_W + 3 * CONV_W

OFF_QR = 0
OFF_KR = OFF_QR + RET_HEADS * RET_DK
OFF_VR = OFF_KR + RET_HEADS * RET_DK
OFF_GR = OFF_VR + RET_W
OFF_QH = OFF_GR + RET_W
OFF_FH = OFF_QH + HG_HEADS * HG_DK
OFF_IH = OFF_FH + HG_HEADS * HG_DK
OFF_GH = OFF_IH + HG_W
OFF_AC = OFF_GH + HG_W
OFF_BC = OFF_AC + CONV_W
OFF_GC = OFF_BC + CONV_W

V7X_VMEM_LIMIT_BYTES = 56 * 1024 * 1024
CHUNK = 128
HIST = 32
DEC_BB = 8

_NT = (((1,), (1,)), ((), ()))
_ANY = pl.BlockSpec(memory_space=pl.ANY)


def _params(sem):
    return pltpu.CompilerParams(dimension_semantics=sem, vmem_limit_bytes=V7X_VMEM_LIMIT_BYTES)


def _sigmoid(x):
    return 0.5 * jnp.tanh(0.5 * x) + 0.5


def _silu(x):
    return x * _sigmoid(x)


def _lend(n_in, carried, out_positions):
    specs, aliases = [], {}
    for buf, out_pos in zip(carried, out_positions):
        if buf is not None:
            aliases[n_in + len(specs)] = out_pos
            specs.append(_ANY)
    return specs, aliases, [b for b in carried if b is not None]


def _rmsnorm_body(x_ref, g_ref, o_ref):
    x = x_ref[...]
    ms = jnp.mean(x * x, axis=-1, keepdims=True)
    o_ref[...] = (x * lax.rsqrt(ms + EPS) * g_ref[...]).astype(o_ref.dtype)


def _rmsnorm(x, g, out_dtype, tm):
    m, d = x.shape
    return pl.pallas_call(
        _rmsnorm_body,
        grid=(m // tm,),
        in_specs=[pl.BlockSpec((tm, d), lambda i: (i, 0)),
                  pl.BlockSpec((1, d), lambda i: (0, 0))],
        out_specs=pl.BlockSpec((tm, d), lambda i: (i, 0)),
        out_shape=jax.ShapeDtypeStruct((m, d), out_dtype),
        compiler_params=_params(("parallel",)),
        name="rmsnorm",
    )(x, g.reshape(1, d))


def _mm_body(x_ref, w_ref, *rest, nk, mode):
    if nk > 1:
        acc_ref = rest[-1]
        rest = rest[:-1]

    def partial_product():
        return jnp.dot(x_ref[...], w_ref[...], preferred_element_type=F32)

    def epilogue(acc):
        if mode == "plain":
            (o_ref,) = rest
            o_ref[...] = acc
        elif mode == "resid":
            r_ref, o_ref, ob_ref = rest
            h1 = r_ref[...] + acc
            o_ref[...] = h1
            ob_ref[...] = h1.astype(BF16)
        else:
            r_ref, p_ref, wp_ref, o_ref = rest
            ple = jnp.dot(p_ref[...], wp_ref[...], preferred_element_type=F32)
            o_ref[...] = r_ref[...] + _sigmoid(acc) * ple

    if nk == 1:
        epilogue(partial_product())
        return
    k = pl.program_id(2)

    @pl.when(k == 0)
    def _():
        acc_ref[...] = partial_product()

    if nk > 2:
        @pl.when(jnp.logical_and(k > 0, k < nk - 1))
        def _():
            acc_ref[...] += partial_product()

    @pl.when(k == nk - 1)
    def _():
        epilogue(acc_ref[...] + partial_product())


def _mm(x, w_all, layer, *, mode, tm, tn, tk, resid=None, p=None, wp_all=None):
    m, kdim = x.shape
    n = w_all.shape[-1]
    nk = kdim // tk
    grid = (m // tm, n // tn, nk)
    in_specs = [pl.BlockSpec((tm, tk), lambda i, j, k: (i, k)),
                pl.BlockSpec((None, tk, tn), lambda i, j, k: (layer, k, j))]
    args = [x, w_all]
    o_spec = pl.BlockSpec((tm, tn), lambda i, j, k: (i, j))
    if mode == "plain":
        out_specs = o_spec
        out_shape = jax.ShapeDtypeStruct((m, n), F32)
    elif mode == "resid":
        in_specs.append(o_spec)
        args.append(resid)
        out_specs = [o_spec, o_spec]
        out_shape = [jax.ShapeDtypeStruct((m, n), F32), jax.ShapeDtypeStruct((m, n), BF16)]
    else:
        pdim = p.shape[-1]
        in_specs += [o_spec,
                     pl.BlockSpec((None, tm, pdim), lambda i, j, k: (layer, i, 0)),
                     pl.BlockSpec((None, pdim, tn), lambda i, j, k: (layer, 0, j))]
        args += [resid, p, wp_all]
        out_specs = o_spec
        out_shape = jax.ShapeDtypeStruct((m, n), F32)
    return pl.pallas_call(
        functools.partial(_mm_body, nk=nk, mode=mode),
        grid=grid,
        in_specs=in_specs,
        out_specs=out_specs,
        out_shape=out_shape,
        scratch_shapes=[pltpu.VMEM((tm, tn), F32)] if nk > 1 else [],
        compiler_params=_params(("parallel", "parallel", "arbitrary")),
        name="mm_" + mode,
    )(*args)


def _ret_prefill_body(q_ref, k_ref, v_ref, g_ref, cos_ref, sin_ref, lg_ref, gn_ref, *rest, n_lent):
    o_ref, s_ref, st_scr = rest[n_lent:]
    c = pl.program_id(1)
    nc = pl.num_programs(1)
    L = CHUNK

    @pl.when(c == 0)
    def _():
        st_scr[...] = jnp.zeros_like(st_scr)

    cos = cos_ref[...]
    sin = sin_ref[...]
    row = lax.broadcasted_iota(jnp.int32, (L, L), 0)
    col = lax.broadcasted_iota(jnp.int32, (L, L), 1)
    diff = (row - col).astype(F32)
    causal = row >= col
    rowf = lax.broadcasted_iota(jnp.int32, (L, RET_DK), 0).astype(F32)
    for h in range(RET_HEADS):
        lg = lg_ref[h:h + 1, :]
        lg1 = lg[:, :RET_DK]
        decay_in = jnp.where(causal, jnp.exp(jnp.where(causal, diff, 0.0) * lg1), 0.0)
        q_decay = jnp.exp((rowf + 1.0) * lg1)
        k_decay = jnp.exp((L - 1.0 - rowf) * lg1)
        chunk_decay = jnp.exp(L * lg)

        zq = q_ref[:, h * RET_DK:(h + 1) * RET_DK]
        zk = k_ref[:, h * RET_DK:(h + 1) * RET_DK]
        v = v_ref[:, h * RET_DV:(h + 1) * RET_DV]
        zg = g_ref[:, h * RET_DV:(h + 1) * RET_DV]
        q = zq * cos + pltpu.roll(zq, RET_DK // 2, 1) * sin
        k = (zk * cos + pltpu.roll(zk, RET_DK // 2, 1) * sin) * (RET_DK ** -0.5)
        vb = v.astype(BF16)
        s0 = st_scr[h]

        scores = lax.dot_general(q.astype(BF16), k.astype(BF16), _NT,
                                 preferred_element_type=F32) * decay_in
        o = (jnp.dot(scores.astype(BF16), vb, preferred_element_type=F32)
             + jnp.dot((q * q_decay).astype(BF16), s0.astype(BF16), preferred_element_type=F32))
        kT = (k * k_decay).T.astype(BF16)
        st_scr[h] = s0 * chunk_decay + jnp.dot(kT, vb, preferred_element_type=F32)

        mu = jnp.mean(o, axis=-1, keepdims=True)
        var = jnp.mean(jnp.square(o - mu), axis=-1, keepdims=True)
        on = (o - mu) * lax.rsqrt(var + EPS) * gn_ref[:, h * RET_DV:(h + 1) * RET_DV]
        o_ref[:, h * RET_DV:(h + 1) * RET_DV] = (on * _silu(zg)).astype(o_ref.dtype)

    @pl.when(c == nc - 1)
    def _():
        s_ref[0] = st_scr[...]


def _ret_prefill(z, cos_t, sin_t, lg_t, gn_g, batch, seq, layer, s_all):
    m = z.shape[0]
    nc = seq // CHUNK
    L = CHUNK
    rowmap = lambda blk: (lambda b, c: (b * nc + c, blk))
    in_specs = [pl.BlockSpec((L, RET_HEADS * RET_DK), rowmap(OFF_QR // (RET_HEADS * RET_DK))),
                pl.BlockSpec((L, RET_HEADS * RET_DK), rowmap(OFF_KR // (RET_HEADS * RET_DK))),
                pl.BlockSpec((L, RET_W), rowmap(OFF_VR // RET_W)),
                pl.BlockSpec((L, RET_W), rowmap(OFF_GR // RET_W)),
                pl.BlockSpec((L, RET_DK), lambda b, c: (c, 0)),
                pl.BlockSpec((L, RET_DK), lambda b, c: (c, 0)),
                pl.BlockSpec((RET_HEADS, RET_DV), lambda b, c: (0, 0)),
                pl.BlockSpec((1, RET_W), lambda b, c: (0, 0))]
    lent_specs, aliases, lent = _lend(len(in_specs), [s_all], [1])
    return pl.pallas_call(
        functools.partial(_ret_prefill_body, n_lent=len(lent)),
        grid=(batch, nc),
        in_specs=in_specs + lent_specs,
        out_specs=[pl.BlockSpec((L, RET_W), lambda b, c: (b * nc + c, 0)),
                   pl.BlockSpec((None, 1, RET_HEADS, RET_DK, RET_DV), lambda b, c: (layer, b, 0, 0, 0))],
        out_shape=[jax.ShapeDtypeStruct((m, D_MODEL), BF16),
                   jax.ShapeDtypeStruct((DEPTH, batch, RET_HEADS, RET_DK, RET_DV), F32)],
        scratch_shapes=[pltpu.VMEM((RET_HEADS, RET_DK, RET_DV), F32)],
        input_output_aliases=aliases,
        compiler_params=_params(("parallel", "arbitrary")),
        name="ret_prefill",
    )(z, z, z, z, cos_t, sin_t, lg_t, gn_g.reshape(1, RET_W), *lent)


HG_HB = 6
_NLEV = int(math.log2(CHUNK))


def _level_table():
    i = np.arange(CHUNK)[:, None]
    j = np.arange(CHUNK)[None, :]
    x = np.bitwise_xor(i, j)
    lvl = np.floor(np.log2(np.maximum(x, 1))).astype(np.int32)
    lvl = np.where(i == j, _NLEV, lvl)
    lvl = np.where(i < j, -1, lvl)
    return lvl.astype(np.int32)


def _split3(x):
    x1 = x.astype(BF16)
    r1 = x - x1.astype(F32)
    x2 = r1.astype(BF16)
    x3 = (r1 - x2.astype(F32)).astype(BF16)
    return x1, x2, x3


def _ref_rows(b, m):
    L, w = b.shape
    g = 2 * m
    b3 = b.reshape(L // g, g, w)
    r = jnp.broadcast_to(b3[:, m - 1:m, :], b3.shape)
    return r.reshape(L, w)


def _hgrn_prefill_body(q_ref, f_ref, i_ref, g_ref, lb_ref, ng_ref, tri_ref, lvl_ref, *rest, n_lent):
    o_ref, s_ref, st_scr = rest[n_lent:]
    c = pl.program_id(2)
    nc = pl.num_programs(2)
    L = CHUNK

    @pl.when(c == 0)
    def _():
        st_scr[...] = jnp.zeros_like(st_scr)

    lvl = lvl_ref[...]
    rowi = lax.broadcasted_iota(jnp.int32, (L, HG_DK), 0)
    odd = (rowi & 1) == 1
    r4 = rowi & 3
    r4_is0, r4_is1, r4_is2 = r4 == 0, r4 == 1, r4 == 2

    lb_all = lb_ref[...]
    f_all = lb_all + (1.0 - lb_all) * _sigmoid(f_ref[...])
    l1, l2, l3 = _split3(jnp.log(f_all))
    b_parts = jnp.dot(tri_ref[...], jnp.concatenate([l1, l2, l3], axis=1), preferred_element_type=F32)
    wblk = HG_HB * HG_DK
    b_all = b_parts[:, 0:wblk] + b_parts[:, wblk:2 * wblk] + b_parts[:, 2 * wblk:]
    for hh in range(HG_HB):
        sl = slice(hh * HG_DK, (hh + 1) * HG_DK)
        v = i_ref[:, sl]
        zg = g_ref[:, sl]
        q = _silu(q_ref[:, sl])
        f = f_all[:, sl]
        k = 1.0 - f
        b = b_all[:, sl]
        qb = q.astype(BF16)
        kb = k.astype(BF16)
        vb = v.astype(BF16)

        a = jnp.where(lvl == _NLEV, lax.dot_general(qb, kb, _NT, preferred_element_type=F32), 0.0)
        f_prev = pltpu.roll(f, 1, 0)
        f_next = pltpu.roll(f, L - 1, 0)
        for t in range(_NLEV):
            m = 1 << t
            if m == 1:
                e = jnp.where(odd, f, 1.0)
            elif m == 2:
                e = jnp.where(r4_is0, f_next, jnp.where(r4_is1, 1.0, jnp.where(r4_is2, f, f * f_prev)))
            else:
                e = jnp.exp(-jnp.abs(b - _ref_rows(b, m)))
            p = lax.dot_general((q * e).astype(BF16), (k * e).astype(BF16), _NT,
                                preferred_element_type=F32)
            a = jnp.where(lvl == t, p, a)

        st = st_scr[hh]
        o = (jnp.dot(a.astype(BF16), vb, preferred_element_type=F32)
             + lax.dot_general((q * jnp.exp(b)).astype(BF16), st.astype(BF16), _NT,
                               preferred_element_type=F32))
        b_last = b[L - 1:L, :]
        k_s = (k * jnp.exp(b_last - b)).astype(BF16)
        st_scr[hh] = st * jnp.exp(b_last) + jnp.dot(v.T.astype(BF16), k_s, preferred_element_type=F32)

        on = o * lax.rsqrt(jnp.mean(o * o, axis=-1, keepdims=True) + EPS) * ng_ref[:, sl]
        o_ref[:, sl] = (on * _silu(zg)).astype(o_ref.dtype)

    @pl.when(c == nc - 1)
    def _():
        for hh in range(HG_HB):
            s_ref[0, hh] = st_scr[hh].T


def _hgrn_prefill(z, mix, lb, ng, batch, seq, layer, s_all):
    nc = seq // CHUNK
    L = CHUNK
    ng_groups = HG_HEADS // HG_HB
    wblk = HG_HB * HG_DK
    rowmap = lambda off: (lambda b, g, c: (b * nc + c, off // wblk + g))
    tri = jnp.asarray(np.tril(np.ones((L, L), np.float32)), dtype=BF16)
    lvl = jnp.asarray(_level_table())
    in_specs = [pl.BlockSpec((L, wblk), rowmap(OFF_QH)),
                pl.BlockSpec((L, wblk), rowmap(OFF_FH)),
                pl.BlockSpec((L, wblk), rowmap(OFF_IH)),
                pl.BlockSpec((L, wblk), rowmap(OFF_GH)),
                pl.BlockSpec((1, wblk), lambda b, g, c: (0, g)),
                pl.BlockSpec((1, wblk), lambda b, g, c: (0, g)),
                pl.BlockSpec((L, L), lambda b, g, c: (0, 0)),
                pl.BlockSpec((L, L), lambda b, g, c: (0, 0))]
    lent_specs, aliases, lent = _lend(len(in_specs), [mix, s_all], [0, 1])
    return pl.pallas_call(
        functools.partial(_hgrn_prefill_body, n_lent=len(lent)),
        grid=(batch, ng_groups, nc),
        in_specs=in_specs + lent_specs,
        out_specs=[pl.BlockSpec((L, wblk), rowmap(RET_W)),
                   pl.BlockSpec((None, 1, HG_HB, HG_DK, HG_DV), lambda b, g, c: (layer, b, g, 0, 0))],
        out_shape=[jax.ShapeDtypeStruct(mix.shape, BF16),
                   jax.ShapeDtypeStruct((DEPTH, batch, HG_HEADS, HG_DK, HG_DV), F32)],
        scratch_shapes=[pltpu.VMEM((HG_HB, HG_DV, HG_DK), F32)],
        input_output_aliases=aliases,
        compiler_params=_params(("parallel", "parallel", "arbitrary")),
        name="hgrn_prefill",
    )(z, z, z, z, lb.reshape(1, HG_W), ng.reshape(1, HG_W), tri, lvl, *lent)


CONV_TT = 256
CONV_RC = 32
_CBLK = 1536


def _conv_tail(y, g, ln_g_ref, ln_b_ref, wpw_ref, bpw_ref):
    mu = jnp.mean(y, axis=-1, keepdims=True)
    var = jnp.mean(jnp.square(y - mu), axis=-1, keepdims=True)
    yn = (y - mu) * lax.rsqrt(var + EPS) * ln_g_ref[...] + ln_b_ref[...]
    pw = jnp.dot(_silu(yn).astype(BF16), wpw_ref[...], preferred_element_type=F32) + bpw_ref[...]
    return pw * _silu(g)


def _conv_prefill_body(z7_ref, z8_ref, wdw_ref, bdw_ref, ln_g_ref, ln_b_ref, wpw_ref, bpw_ref,
                       *rest, n_lent):
    o_ref, s_ref, ext, shifted, ybuf = rest[n_lent:]
    t = pl.program_id(1)
    nt = pl.num_programs(1)
    tt = CONV_TT
    sub = 8

    @pl.when(t == 0)
    def _():
        ext[0:HIST, :] = jnp.zeros((HIST, CONV_W), F32)

    a = z7_ref[:, 0:CONV_W]
    bgate = jnp.concatenate([z7_ref[:, CONV_W:_CBLK], z8_ref[:, 0:2 * CONV_W - _CBLK]], axis=-1)
    g = z8_ref[:, 2 * CONV_W - _CBLK:]
    ext[HIST:HIST + tt, :] = a * _sigmoid(bgate)

    base = HIST - (CONV_K - 1)
    n_sh = HIST + tt - sub
    for r in range(1, sub):
        shifted[r, 0:n_sh, :] = ext[r:r + n_sh, :]
    for ci in range(tt // CONV_RC):
        r0 = ci * CONV_RC
        acc = jnp.zeros((CONV_RC // sub, sub, CONV_W), F32)
        for k in range(CONV_K):
            r = (base + k) % sub
            a0 = r0 + (base + k) - r
            rows = ext[a0:a0 + CONV_RC, :] if r == 0 else shifted[r, a0:a0 + CONV_RC, :]
            acc = acc + wdw_ref[k][None] * rows.reshape(CONV_RC // sub, sub, CONV_W)
        ybuf[r0:r0 + CONV_RC, :] = acc.reshape(CONV_RC, CONV_W)
    y = ybuf[...] + bdw_ref[...]
    o_ref[...] = _conv_tail(y, g, ln_g_ref, ln_b_ref, wpw_ref, bpw_ref).astype(o_ref.dtype)

    @pl.when(t == nt - 1)
    def _():
        s_ref[0] = ext[tt + HIST - (CONV_K - 1):tt + HIST, :]

    ext[0:HIST, :] = ext[tt:tt + HIST, :]


def _conv_prefill(z, mix, w_dw, b_dw, ln_g, ln_b, wpw_all, b_pw, batch, seq, layer, s_all):
    tt = CONV_TT
    nt = seq // tt
    vec = lambda i, t: (0, 0)
    in_specs = [pl.BlockSpec((tt, _CBLK), lambda b, t: (b * nt + t, OFF_AC // _CBLK)),
                pl.BlockSpec((tt, _CBLK), lambda b, t: (b * nt + t, OFF_AC // _CBLK + 1)),
                pl.BlockSpec((CONV_K, 8, CONV_W), lambda b, t: (0, 0, 0)),
                pl.BlockSpec((1, CONV_W), vec),
                pl.BlockSpec((1, CONV_W), vec),
                pl.BlockSpec((1, CONV_W), vec),
                pl.BlockSpec((None, CONV_W, CONV_W), lambda b, t: (layer, 0, 0)),
                pl.BlockSpec((1, CONV_W), vec)]
    lent_specs, aliases, lent = _lend(len(in_specs), [mix, s_all], [0, 1])
    return pl.pallas_call(
        functools.partial(_conv_prefill_body, n_lent=len(lent)),
        grid=(batch, nt),
        in_specs=in_specs + lent_specs,
        out_specs=[pl.BlockSpec((tt, CONV_W), lambda b, t: (b * nt + t, (RET_W + HG_W) // CONV_W)),
                   pl.BlockSpec((None, 1, CONV_K - 1, CONV_W), lambda b, t: (layer, b, 0, 0))],
        out_shape=[jax.ShapeDtypeStruct(mix.shape, BF16),
                   jax.ShapeDtypeStruct((DEPTH, batch, CONV_K - 1, CONV_W), F32)],
        scratch_shapes=[pltpu.VMEM((HIST + tt, CONV_W), F32), pltpu.VMEM((8, HIST + tt, CONV_W), F32),
                        pltpu.VMEM((tt, CONV_W), F32)],
        input_output_aliases=aliases,
        compiler_params=_params(("parallel", "arbitrary")),
        name="conv_prefill",
    )(z, z, jnp.broadcast_to(w_dw[:, None, :], (CONV_K, 8, CONV_W)), b_dw.reshape(1, -1),
      ln_g.reshape(1, -1), ln_b.reshape(1, -1), wpw_all, b_pw.reshape(1, -1), *lent)


def _ret_decode_body(qT_ref, kT_ref, v_ref, g_ref, cos_ref, sin_ref, lg_ref, gn_ref, s_ref, *rest,
                     n_lent):
    o_ref, so_ref, orow = rest[n_lent:]
    half = RET_DK // 2
    cosc = cos_ref[...]
    sinc = sin_ref[...]
    for h in range(RET_HEADS):
        zq = qT_ref[h]
        zk = kT_ref[h]
        q = zq * cosc + jnp.concatenate([zq[half:], zq[:half]], axis=0) * sinc
        k = (zk * cosc + jnp.concatenate([zk[half:], zk[:half]], axis=0) * sinc) * (RET_DK ** -0.5)
        gamma = jnp.exp(lg_ref[h:h + 1, :])
        for j in range(DEC_BB):
            vrow = v_ref[j:j + 1, h * RET_DV:(h + 1) * RET_DV]
            s1 = gamma * s_ref[j, h] + k[:, j:j + 1] * vrow
            so_ref[j, h] = s1
            orow[j:j + 1, h * RET_DV:(h + 1) * RET_DV] = jnp.sum(q[:, j:j + 1] * s1, axis=0, keepdims=True)
    for h in range(RET_HEADS):
        sl = slice(h * RET_DV, (h + 1) * RET_DV)
        o = orow[:, sl]
        mu = jnp.mean(o, axis=-1, keepdims=True)
        var = jnp.mean(jnp.square(o - mu), axis=-1, keepdims=True)
        on = (o - mu) * lax.rsqrt(var + EPS) * gn_ref[:, sl]
        o_ref[:, sl] = on * _silu(g_ref[:, sl])


def _cols(zs, off, heads, dk):
    bsz = zs.shape[0]
    x = zs[:, off:off + heads * dk].reshape(bsz // DEC_BB, DEC_BB, heads, dk)
    return x.transpose(0, 2, 3, 1)


def _ret_decode(zs, state_all, layer, cosc, sinc, lg_t, gn_g, so_all):
    bsz = zs.shape[0]
    bb = DEC_BB
    qT = _cols(zs, OFF_QR, RET_HEADS, RET_DK)
    kT = _cols(zs, OFF_KR, RET_HEADS, RET_DK)
    col_spec = pl.BlockSpec((None, RET_HEADS, RET_DK, bb), lambda i: (i, 0, 0, 0))
    st_spec = pl.BlockSpec((None, bb, RET_HEADS, RET_DK, RET_DV), lambda i: (layer, i, 0, 0, 0))
    in_specs = [col_spec, col_spec,
                pl.BlockSpec((bb, RET_W), lambda i: (i, OFF_VR // RET_W)),
                pl.BlockSpec((bb, RET_W), lambda i: (i, OFF_GR // RET_W)),
                pl.BlockSpec((RET_DK, bb), lambda i: (0, 0)),
                pl.BlockSpec((RET_DK, bb), lambda i: (0, 0)),
                pl.BlockSpec((RET_HEADS, RET_DV), lambda i: (0, 0)),
                pl.BlockSpec((1, RET_W), lambda i: (0, 0)),
                st_spec]
    lent_specs, aliases, lent = _lend(len(in_specs), [so_all], [1])
    return pl.pallas_call(
        functools.partial(_ret_decode_body, n_lent=len(lent)),
        grid=(bsz // bb,),
        in_specs=in_specs + lent_specs,
        out_specs=[pl.BlockSpec((bb, RET_W), lambda i: (i, 0)), st_spec],
        out_shape=[jax.ShapeDtypeStruct((bsz, RET_W), F32),
                   jax.ShapeDtypeStruct(state_all.shape, F32)],
        scratch_shapes=[pltpu.VMEM((bb, RET_W), F32)],
        input_output_aliases=aliases,
        compiler_params=_params(("parallel",)),
        name="ret_decode",
    )(qT, kT, zs, zs, cosc, sinc, lg_t, gn_g.reshape(1, RET_W), state_all, *lent)


def _hgrn_decode_body(q_ref, fT_ref, v_ref, g_ref, lbT_ref, ng_ref, s_ref, *rest, n_lent):
    o_ref, so_ref, orow = rest[n_lent:]
    for h in range(HG_HEADS):
        sl = slice(h * HG_DV, (h + 1) * HG_DV)
        qb = _silu(q_ref[:, sl]).astype(BF16)
        lb = lbT_ref[h]
        f = lb + (1.0 - lb) * _sigmoid(fT_ref[h])
        for j in range(DEC_BB):
            vrow = v_ref[j:j + 1, sl]
            s1 = f[:, j:j + 1] * (s_ref[j, h] - vrow) + vrow
            so_ref[j, h] = s1
            oj = jnp.dot(qb, s1.astype(BF16), preferred_element_type=F32)
            orow[j:j + 1, sl] = oj[j:j + 1, :]
    for h in range(HG_HEADS):
        sl = slice(h * HG_DV, (h + 1) * HG_DV)
        o = orow[:, sl]
        on = o * lax.rsqrt(jnp.mean(o * o, axis=-1, keepdims=True) + EPS) * ng_ref[:, sl]
        o_ref[:, sl] = on * _silu(g_ref[:, sl])


def _hgrn_decode(zs, state_all, layer, lb, ng, so_all):
    bsz = zs.shape[0]
    bb = DEC_BB
    fT = _cols(zs, OFF_FH, HG_HEADS, HG_DK)
    col_spec = pl.BlockSpec((None, HG_HEADS, HG_DK, bb), lambda i: (i, 0, 0, 0))
    st_spec = pl.BlockSpec((None, bb, HG_HEADS, HG_DK, HG_DV), lambda i: (layer, i, 0, 0, 0))
    in_specs = [pl.BlockSpec((bb, HG_W), lambda i: (i, OFF_QH // HG_W)),
                col_spec,
                pl.BlockSpec((bb, HG_W), lambda i: (i, OFF_IH // HG_W)),
                pl.BlockSpec((bb, HG_W), lambda i: (i, OFF_GH // HG_W)),
                pl.BlockSpec((HG_HEADS, HG_DK, 1), lambda i: (0, 0, 0)),
                pl.BlockSpec((1, HG_W), lambda i: (0, 0)),
                st_spec]
    lent_specs, aliases, lent = _lend(len(in_specs), [so_all], [1])
    return pl.pallas_call(
        functools.partial(_hgrn_decode_body, n_lent=len(lent)),
        grid=(bsz // bb,),
        in_specs=in_specs + lent_specs,
        out_specs=[pl.BlockSpec((bb, HG_W), lambda i: (i, 0)), st_spec],
        out_shape=[jax.ShapeDtypeStruct((bsz, HG_W), F32),
                   jax.ShapeDtypeStruct(state_all.shape, F32)],
        scratch_shapes=[pltpu.VMEM((bb, HG_W), F32)],
        input_output_aliases=aliases,
        compiler_params=_params(("parallel",)),
        name="hgrn_decode",
    )(zs, fT, zs, zs, lb.reshape(HG_HEADS, HG_DK, 1), ng.reshape(1, HG_W), state_all, *lent)


def _conv_decode_body(z7_ref, z8_ref, wdw_ref, bdw_ref, ln_g_ref, ln_b_ref, wpw_ref, bpw_ref, s_ref,
                      *rest, n_lent):
    o_ref, so_ref = rest[n_lent:]
    kh = CONV_K - 1
    a = z7_ref[:, 0:CONV_W]
    bgate = jnp.concatenate([z7_ref[:, CONV_W:_CBLK], z8_ref[:, 0:2 * CONV_W - _CBLK]], axis=-1)
    g = z8_ref[:, 2 * CONV_W - _CBLK:]
    u = a * _sigmoid(bgate)
    buf = s_ref[...]
    y = jnp.sum(buf * wdw_ref[0:kh, :][None], axis=1) + u * wdw_ref[kh:kh + 1, :] + bdw_ref[...]
    o_ref[...] = _conv_tail(y, g, ln_g_ref, ln_b_ref, wpw_ref, bpw_ref)
    so_ref[:, 0:kh - 1, :] = s_ref[:, 1:kh, :]
    for j in range(DEC_BB):
        so_ref[j, kh - 1:kh, :] = u[j:j + 1, :]


def _conv_decode(zs, state_all, layer, w_dw, b_dw, ln_g, ln_b, wpw_all, b_pw, so_all):
    bsz = zs.shape[0]
    bb = DEC_BB
    vec = lambda i: (0, 0)
    st_spec = pl.BlockSpec((None, bb, CONV_K - 1, CONV_W), lambda i: (layer, i, 0, 0))
    in_specs = [pl.BlockSpec((bb, _CBLK), lambda i: (i, OFF_AC // _CBLK)),
                pl.BlockSpec((bb, _CBLK), lambda i: (i, OFF_AC // _CBLK + 1)),
                pl.BlockSpec((CONV_K, CONV_W), vec),
                pl.BlockSpec((1, CONV_W), vec),
                pl.BlockSpec((1, CONV_W), vec),
                pl.BlockSpec((1, CONV_W), vec),
                pl.BlockSpec((None, CONV_W, CONV_W), lambda i: (layer, 0, 0)),
                pl.BlockSpec((1, CONV_W), vec),
                st_spec]
    lent_specs, aliases, lent = _lend(len(in_specs), [so_all], [1])
    return pl.pallas_call(
        functools.partial(_conv_decode_body, n_lent=len(lent)),
        grid=(bsz // bb,),
        in_specs=in_specs + lent_specs,
        out_specs=[pl.BlockSpec((bb, CONV_W), lambda i: (i, 0)), st_spec],
        out_shape=[jax.ShapeDtypeStruct((bsz, CONV_W), F32),
                   jax.ShapeDtypeStruct(state_all.shape, F32)],
        input_output_aliases=aliases,
        compiler_params=_params(("parallel",)),
        name="conv_decode",
    )(zs, zs, w_dw, b_dw.reshape(1, -1), ln_g.reshape(1, -1), ln_b.reshape(1, -1), wpw_all,
      b_pw.reshape(1, -1), state_all, *lent)


def _rope_tables(pos):
    half = RET_DK // 2
    inv = ROPE_BASE ** (-jnp.arange(half, dtype=F32) / half)
    ang = pos[:, None] * inv[None, :]
    cos = jnp.cos(ang)
    sin = jnp.sin(ang)
    return jnp.concatenate([cos, cos], axis=-1), jnp.concatenate([-sin, sin], axis=-1)


def _tiles(m):
    if m >= 1024:
        return 1024, 1536, 1024, 2048
    return m, 1536, 1024, D_MODEL


def kernel(x_prompt, x_sample, state_ret, state_hgrn, state_conv, p_prompt, p_sample, norm_g, w_in,
           ret_gn_g, hg_lower_bounds, hg_norm_g, w_dw, b_dw, conv_ln_g, conv_ln_b, w_pw, b_pw, w_out,
           w_ple, w_pg, final_norm_g):
    bp, tp, d = x_prompt.shape
    bs, ts, _ = x_sample.shape
    assert ts == 1 and tp % CONV_TT == 0 and bs % DEC_BB == 0

    w_in_b = w_in.astype(BF16)
    w_out_b = w_out.astype(BF16)
    w_pg_b = w_pg.astype(BF16)
    w_ple_b = w_ple.astype(BF16)
    w_pw_b = w_pw.astype(BF16)

    lbs = jax.nn.softmax(hg_lower_bounds.astype(F32), axis=0)
    lbs = jnp.cumsum(lbs, axis=0) - lbs[0:1]
    log_gamma = np.log(1.0 - 2.0 ** (-5.0 - np.arange(RET_HEADS, dtype=np.float32))).astype(np.float32)
    lg_t = jnp.asarray(np.broadcast_to(log_gamma[:, None], (RET_HEADS, RET_DV)))

    cos_p, sin_p = _rope_tables(jnp.arange(tp, dtype=F32))
    cos_s, sin_s = _rope_tables(PAST_LEN + jnp.arange(ts, dtype=F32))
    cosc = jnp.broadcast_to(cos_s.reshape(RET_DK, 1), (RET_DK, DEC_BB))
    sinc = jnp.broadcast_to(sin_s.reshape(RET_DK, 1), (RET_DK, DEC_BB))

    def layer(h, l, p_b, prompt, states):
        s_r, s_h, s_c = states
        m = h.shape[0]
        tm, tn_in, tn_sq, tk = _tiles(m)
        xn = _rmsnorm(h, norm_g[l], BF16, min(m, 256))
        z = _mm(xn, w_in_b, l, mode="plain", tm=tm, tn=tn_in, tk=tk)
        if prompt:
            mix, s_r = _ret_prefill(z, cos_p, sin_p, lg_t, ret_gn_g[l], bp, tp, l, s_r)
            mix, s_h = _hgrn_prefill(z, mix, lbs[l], hg_norm_g[l], bp, tp, l, s_h)
            mix, s_c = _conv_prefill(z, mix, w_dw[l], b_dw[l], conv_ln_g[l], conv_ln_b[l], w_pw_b,
                                     b_pw[l], bp, tp, l, s_c)
        else:
            mix_r, s_r = _ret_decode(z, state_ret, l, cosc, sinc, lg_t, ret_gn_g[l], s_r)
            mix_h, s_h = _hgrn_decode(z, state_hgrn, l, lbs[l], hg_norm_g[l], s_h)
            mix_c, s_c = _conv_decode(z, state_conv, l, w_dw[l], b_dw[l], conv_ln_g[l], conv_ln_b[l],
                                      w_pw_b, b_pw[l], s_c)
            mix = jnp.concatenate([mix_r, mix_h, mix_c], axis=-1).astype(BF16)
        h1, h1b = _mm(mix, w_out_b, l, mode="resid", tm=tm, tn=tn_sq, tk=tk, resid=h)
        h2 = _mm(h1b, w_pg_b, l, mode="ple", tm=tm, tn=tn_sq, tk=tk, resid=h1, p=p_b, wp_all=w_ple_b)
        return h2, (s_r, s_h, s_c)

    def trunk(x, p, prompt):
        m = x.shape[0] * x.shape[1]
        h = x.reshape(m, d)
        p_b = p.reshape(DEPTH, m, PLE_DIM).astype(BF16)
        states = (None, None, None)
        for l in range(DEPTH):
            h, states = layer(h, l, p_b, prompt, states)
        y = _rmsnorm(h, final_norm_g, F32, min(m, 256)).reshape(x.shape)
        return (y,) + states

    y_p, ret_p, hg_p, conv_p = trunk(x_prompt, p_prompt, True)
    y_s, ret_s, hg_s, conv_s = trunk(x_sample, p_sample, False)
    return (y_p, y_s, ret_p, hg_p, conv_p, ret_s, hg_s, conv_s)
```

```python
import functools
import math

import numpy as np
import jax
import jax.numpy as jnp
from jax import lax
from jax.experimental import pallas as pl
from jax.experimental.pallas import tpu as pltpu

F32 = jnp.float32
BF16 = jnp.bfloat16

D_MODEL = 4096
DEPTH = 4
PAST_LEN = 16384
PLE_DIM = 256
RET_HEADS = 6
RET_DK = 128
RET_DV = 256
RET_W = RET_HEADS * RET_DV
ROPE_BASE = 10000.0
HG_HEADS = 12
HG_DK = 128
HG_DV = 128
HG_W = HG_HEADS * HG_DV
CONV_W = D_MODEL - RET_W - HG_W
CONV_K = 31
EPS = 1e-6
IN_W = 2 * RET_HEADS * RET_DK + 2 * RET_W + 2 * HG_HEADS * HG_DK + 2 * HG_W + 3 * CONV_W

OFF_QR = 0
OFF_KR = OFF_QR + RET_HEADS * RET_DK
OFF_VR = OFF_KR + RET_HEADS * RET_DK
OFF_GR = OFF_VR + RET_W
OFF_QH = OFF_GR + RET_W
OFF_FH = OFF_QH + HG_HEADS * HG_DK
OFF_IH = OFF_FH + HG_HEADS * HG_DK
OFF_GH = OFF_IH + HG_W
OFF_AC = OFF_GH + HG_W
OFF_BC = OFF_AC + CONV_W
OFF_GC = OFF_BC + CONV_W

V7X_VMEM_LIMIT_BYTES = 56 * 1024 * 1024
CHUNK = 128
HIST = 32
DEC_BB = 8

_NT = (((1,), (1,)), ((), ()))
_ANY = pl.BlockSpec(memory_space=pl.ANY)


def _params(sem):
    return pltpu.CompilerParams(dimension_semantics=sem, vmem_limit_bytes=V7X_VMEM_LIMIT_BYTES)


def _sigmoid(x):
    return 0.5 * jnp.tanh(0.5 * x) + 0.5


def _silu(x):
    return x * _sigmoid(x)


def _lend(n_in, carried, out_positions):
    specs, aliases = [], {}
    for buf, out_pos in zip(carried, out_positions):
        if buf is not None:
            aliases[n_in + len(specs)] = out_pos
            specs.append(_ANY)
    return specs, aliases, [b for b in carried if b is not None]


def _rmsnorm_body(x_ref, g_ref, o_ref):
    x = x_ref[...]
    ms = jnp.mean(x * x, axis=-1, keepdims=True)
    o_ref[...] = (x * lax.rsqrt(ms + EPS) * g_ref[...]).astype(o_ref.dtype)


def _rmsnorm(x, g, out_dtype, tm):
    m, d = x.shape
    return pl.pallas_call(
        _rmsnorm_body,
        grid=(m // tm,),
        in_specs=[pl.BlockSpec((tm, d), lambda i: (i, 0)),
                  pl.BlockSpec((1, d), lambda i: (0, 0))],
        out_specs=pl.BlockSpec((tm, d), lambda i: (i, 0)),
        out_shape=jax.ShapeDtypeStruct((m, d), out_dtype),
        compiler_params=_params(("parallel",)),
        name="rmsnorm",
    )(x, g.reshape(1, d))


def _mm_body(x_ref, w_ref, *rest, nk, mode, cast_w):
    if nk > 1:
        acc_ref = rest[-1]
        rest = rest[:-1]
    if cast_w:
        wb_ref = rest[-1]
        rest = rest[:-1]

    def partial_product():
        w = w_ref[...]
        if cast_w:
            w = w.astype(BF16)
            wb_ref[...] = w
        return jnp.dot(x_ref[...], w, preferred_element_type=F32)

    def epilogue(acc):
        if mode == "plain":
            (o_ref,) = rest
            o_ref[...] = acc
        elif mode == "resid":
            r_ref, o_ref, ob_ref = rest
            h1 = r_ref[...] + acc
            o_ref[...] = h1
            ob_ref[...] = h1.astype(BF16)
        else:
            r_ref, p_ref, wp_ref, o_ref = rest
            ple = jnp.dot(p_ref[...], wp_ref[...], preferred_element_type=F32)
            o_ref[...] = r_ref[...] + _sigmoid(acc) * ple

    if nk == 1:
        epilogue(partial_product())
        return
    k = pl.program_id(2)

    @pl.when(k == 0)
    def _():
        acc_ref[...] = partial_product()

    if nk > 2:
        @pl.when(jnp.logical_and(k > 0, k < nk - 1))
        def _():
            acc_ref[...] += partial_product()

    @pl.when(k == nk - 1)
    def _():
        epilogue(acc_ref[...] + partial_product())


def _mm(x, w, layer, *, mode, tm, tn, tk, resid=None, p=None, wp_all=None):
    m, kdim = x.shape
    n = w.shape[-1]
    nk = kdim // tk
    cast_w = w.ndim == 3
    assert not cast_w or (nk == 1 and m == tm)
    grid = (m // tm, n // tn, nk)
    w_spec = (pl.BlockSpec((None, tk, tn), lambda i, j, k: (layer, k, j)) if cast_w
              else pl.BlockSpec((tk, tn), lambda i, j, k: (k, j)))
    in_specs = [pl.BlockSpec((tm, tk), lambda i, j, k: (i, k)), w_spec]
    args = [x, w]
    o_spec = pl.BlockSpec((tm, tn), lambda i, j, k: (i, j))
    out_specs = [o_spec]
    out_shape = [jax.ShapeDtypeStruct((m, n), F32)]
    if mode == "resid":
        in_specs.append(o_spec)
        args.append(resid)
        out_specs.append(o_spec)
        out_shape.append(jax.ShapeDtypeStruct((m, n), BF16))
    elif mode == "ple":
        pdim = p.shape[-1]
        in_specs += [o_spec,
                     pl.BlockSpec((None, tm, pdim), lambda i, j, k: (layer, i, 0)),
                     pl.BlockSpec((None, pdim, tn), lambda i, j, k: (layer, 0, j))]
        args += [resid, p, wp_all]
    if cast_w:
        out_specs.append(pl.BlockSpec((tk, tn), lambda i, j, k: (k, j)))
        out_shape.append(jax.ShapeDtypeStruct((kdim, n), BF16))
    return pl.pallas_call(
        functools.partial(_mm_body, nk=nk, mode=mode, cast_w=cast_w),
        grid=grid,
        in_specs=in_specs,
        out_specs=out_specs,
        out_shape=out_shape,
        scratch_shapes=[pltpu.VMEM((tm, tn), F32)] if nk > 1 else [],
        compiler_params=_params(("parallel", "parallel", "arbitrary")),
        name="mm_" + mode + ("_cast" if cast_w else ""),
    )(*args)


def _ret_prefill_body(q_ref, k_ref, v_ref, g_ref, cos_ref, sin_ref, lg_ref, gn_ref, *rest, n_lent):
    o_ref, s_ref, st_scr = rest[n_lent:]
    c = pl.program_id(1)
    nc = pl.num_programs(1)
    L = CHUNK

    @pl.when(c == 0)
    def _():
        st_scr[...] = jnp.zeros_like(st_scr)

    cos = cos_ref[...]
    sin = sin_ref[...]
    row = lax.broadcasted_iota(jnp.int32, (L, L), 0)
    col = lax.broadcasted_iota(jnp.int32, (L, L), 1)
    diff = (row - col).astype(F32)
    causal = row >= col
    rowf = lax.broadcasted_iota(jnp.int32, (L, RET_DK), 0).astype(F32)
    for h in range(RET_HEADS):
        lg = lg_ref[h:h + 1, :]
        lg1 = lg[:, :RET_DK]
        decay_in = jnp.where(causal, jnp.exp(jnp.where(causal, diff, 0.0) * lg1), 0.0)
        q_decay = jnp.exp((rowf + 1.0) * lg1)
        k_decay = jnp.exp((L - 1.0 - rowf) * lg1)
        chunk_decay = jnp.exp(L * lg)

        zq = q_ref[:, h * RET_DK:(h + 1) * RET_DK]
        zk = k_ref[:, h * RET_DK:(h + 1) * RET_DK]
        v = v_ref[:, h * RET_DV:(h + 1) * RET_DV]
        zg = g_ref[:, h * RET_DV:(h + 1) * RET_DV]
        q = zq * cos + pltpu.roll(zq, RET_DK // 2, 1) * sin
        k = (zk * cos + pltpu.roll(zk, RET_DK // 2, 1) * sin) * (RET_DK ** -0.5)
        vb = v.astype(BF16)
        s0 = st_scr[h]

        scores = lax.dot_general(q.astype(BF16), k.astype(BF16), _NT,
                                 preferred_element_type=F32) * decay_in
        o = (jnp.dot(scores.astype(BF16), vb, preferred_element_type=F32)
             + jnp.dot((q * q_decay).astype(BF16), s0.astype(BF16), preferred_element_type=F32))
        kT = (k * k_decay).T.astype(BF16)
        st_scr[h] = s0 * chunk_decay + jnp.dot(kT, vb, preferred_element_type=F32)

        mu = jnp.mean(o, axis=-1, keepdims=True)
        var = jnp.mean(jnp.square(o - mu), axis=-1, keepdims=True)
        on = (o - mu) * lax.rsqrt(var + EPS) * gn_ref[:, h * RET_DV:(h + 1) * RET_DV]
        o_ref[:, h * RET_DV:(h + 1) * RET_DV] = (on * _silu(zg)).astype(o_ref.dtype)

    @pl.when(c == nc - 1)
    def _():
        s_ref[0] = st_scr[...]


def _ret_prefill(z, cos_t, sin_t, lg_t, gn_g, batch, seq, layer, s_all):
    m = z.shape[0]
    nc = seq // CHUNK
    L = CHUNK
    rowmap = lambda blk: (lambda b, c: (b * nc + c, blk))
    in_specs = [pl.BlockSpec((L, RET_HEADS * RET_DK), rowmap(OFF_QR // (RET_HEADS * RET_DK))),
                pl.BlockSpec((L, RET_HEADS * RET_DK), rowmap(OFF_KR // (RET_HEADS * RET_DK))),
                pl.BlockSpec((L, RET_W), rowmap(OFF_VR // RET_W)),
                pl.BlockSpec((L, RET_W), rowmap(OFF_GR // RET_W)),
                pl.BlockSpec((L, RET_DK), lambda b, c: (c, 0)),
                pl.BlockSpec((L, RET_DK), lambda b, c: (c, 0)),
                pl.BlockSpec((RET_HEADS, RET_DV), lambda b, c: (0, 0)),
                pl.BlockSpec((1, RET_W), lambda b, c: (0, 0))]
    lent_specs, aliases, lent = _lend(len(in_specs), [s_all], [1])
    return pl.pallas_call(
        functools.partial(_ret_prefill_body, n_lent=len(lent)),
        grid=(batch, nc),
        in_specs=in_specs + lent_specs,
        out_specs=[pl.BlockSpec((L, RET_W), lambda b, c: (b * nc + c, 0)),
                   pl.BlockSpec((None, 1, RET_HEADS, RET_DK, RET_DV), lambda b, c: (layer, b, 0, 0, 0))],
        out_shape=[jax.ShapeDtypeStruct((m, D_MODEL), BF16),
                   jax.ShapeDtypeStruct((DEPTH, batch, RET_HEADS, RET_DK, RET_DV), F32)],
        scratch_shapes=[pltpu.VMEM((RET_HEADS, RET_DK, RET_DV), F32)],
        input_output_aliases=aliases,
        compiler_params=_params(("parallel", "arbitrary")),
        name="ret_prefill",
    )(z, z, z, z, cos_t, sin_t, lg_t, gn_g.reshape(1, RET_W), *lent)


HG_HB = 12
_NLEV = int(math.log2(CHUNK))


def _level_table():
    i = np.arange(CHUNK)[:, None]
    j = np.arange(CHUNK)[None, :]
    x = np.bitwise_xor(i, j)
    lvl = np.floor(np.log2(np.maximum(x, 1))).astype(np.int32)
    lvl = np.where(i == j, _NLEV, lvl)
    lvl = np.where(i < j, -1, lvl)
    return lvl.astype(np.int32)


def _split3(x):
    x1 = x.astype(BF16)
    r1 = x - x1.astype(F32)
    x2 = r1.astype(BF16)
    x3 = (r1 - x2.astype(F32)).astype(BF16)
    return x1, x2, x3


def _ref_rows(b, m):
    L, w = b.shape
    g = 2 * m
    b3 = b.reshape(L // g, g, w)
    r = jnp.broadcast_to(b3[:, m - 1:m, :], b3.shape)
    return r.reshape(L, w)


def _hgrn_prefill_body(q_ref, f_ref, i_ref, g_ref, lb_ref, ng_ref, tri_ref, lvl_ref, *rest, n_lent):
    o_ref, s_ref, st_scr = rest[n_lent:]
    c = pl.program_id(2)
    nc = pl.num_programs(2)
    L = CHUNK

    @pl.when(c == 0)
    def _():
        st_scr[...] = jnp.zeros_like(st_scr)

    lvl = lvl_ref[...]
    rowi = lax.broadcasted_iota(jnp.int32, (L, HG_DK), 0)
    odd = (rowi & 1) == 1
    r4 = rowi & 3
    r4_is0, r4_is1, r4_is2 = r4 == 0, r4 == 1, r4 == 2

    lb_all = lb_ref[...]
    f_all = lb_all + (1.0 - lb_all) * _sigmoid(f_ref[...])
    l1, l2, l3 = _split3(jnp.log(f_all))
    b_parts = jnp.dot(tri_ref[...], jnp.concatenate([l1, l2, l3], axis=1), preferred_element_type=F32)
    wblk = HG_HB * HG_DK
    b_all = b_parts[:, 0:wblk] + b_parts[:, wblk:2 * wblk] + b_parts[:, 2 * wblk:]
    for hh in range(HG_HB):
        sl = slice(hh * HG_DK, (hh + 1) * HG_DK)
        v = i_ref[:, sl]
        zg = g_ref[:, sl]
        q = _silu(q_ref[:, sl])
        f = f_all[:, sl]
        k = 1.0 - f
        b = b_all[:, sl]
        qb = q.astype(BF16)
        kb = k.astype(BF16)
        vb = v.astype(BF16)

        a = jnp.where(lvl == _NLEV, lax.dot_general(qb, kb, _NT, preferred_element_type=F32), 0.0)
        f_prev = pltpu.roll(f, 1, 0)
        f_next = pltpu.roll(f, L - 1, 0)
        for t in range(_NLEV):
            m = 1 << t
            if m == 1:
                e = jnp.where(odd, f, 1.0)
            elif m == 2:
                e = jnp.where(r4_is0, f_next, jnp.where(r4_is1, 1.0, jnp.where(r4_is2, f, f * f_prev)))
            else:
                e = jnp.exp(-jnp.abs(b - _ref_rows(b, m)))
            p = lax.dot_general((q * e).astype(BF16), (k * e).astype(BF16), _NT,
                                preferred_element_type=F32)
            a = jnp.where(lvl == t, p, a)

        st = st_scr[hh]
        o = (jnp.dot(a.astype(BF16), vb, preferred_element_type=F32)
             + lax.dot_general((q * jnp.exp(b)).astype(BF16), st.astype(BF16), _NT,
                               preferred_element_type=F32))
        b_last = b[L - 1:L, :]
        k_s = (k * jnp.exp(b_last - b)).astype(BF16)
        st_scr[hh] = st * jnp.exp(b_last) + jnp.dot(v.T.astype(BF16), k_s, preferred_element_type=F32)

        on = o * lax.rsqrt(jnp.mean(o * o, axis=-1, keepdims=True) + EPS) * ng_ref[:, sl]
        o_ref[:, sl] = (on * _silu(zg)).astype(o_ref.dtype)

    @pl.when(c == nc - 1)
    def _():
        for hh in range(HG_HB):
            s_ref[0, hh] = st_scr[hh].T


def _hgrn_prefill(z, mix, lb, ng, batch, seq, layer, s_all):
    nc = seq // CHUNK
    L = CHUNK
    ng_groups = HG_HEADS // HG_HB
    wblk = HG_HB * HG_DK
    rowmap = lambda off: (lambda b, g, c: (b * nc + c, off // wblk + g))
    tri = jnp.asarray(np.tril(np.ones((L, L), np.float32)), dtype=BF16)
    lvl = jnp.asarray(_level_table())
    in_specs = [pl.BlockSpec((L, wblk), rowmap(OFF_QH)),
                pl.BlockSpec((L, wblk), rowmap(OFF_FH)),
                pl.BlockSpec((L, wblk), rowmap(OFF_IH)),
                pl.BlockSpec((L, wblk), rowmap(OFF_GH)),
                pl.BlockSpec((1, wblk), lambda b, g, c: (0, g)),
                pl.BlockSpec((1, wblk), lambda b, g, c: (0, g)),
                pl.BlockSpec((L, L), lambda b, g, c: (0, 0)),
                pl.BlockSpec((L, L), lambda b, g, c: (0, 0))]
    lent_specs, aliases, lent = _lend(len(in_specs), [mix, s_all], [0, 1])
    return pl.pallas_call(
        functools.partial(_hgrn_prefill_body, n_lent=len(lent)),
        grid=(batch, ng_groups, nc),
        in_specs=in_specs + lent_specs,
        out_specs=[pl.BlockSpec((L, wblk), rowmap(RET_W)),
                   pl.BlockSpec((None, 1, HG_HB, HG_DK, HG_DV), lambda b, g, c: (layer, b, g, 0, 0))],
        out_shape=[jax.ShapeDtypeStruct(mix.shape, BF16),
                   jax.ShapeDtypeStruct((DEPTH, batch, HG_HEADS, HG_DK, HG_DV), F32)],
        scratch_shapes=[pltpu.VMEM((HG_HB, HG_DV, HG_DK), F32)],
        input_output_aliases=aliases,
        compiler_params=_params(("parallel", "parallel", "arbitrary")),
        name="hgrn_prefill",
    )(z, z, z, z, lb.reshape(1, HG_W), ng.reshape(1, HG_W), tri, lvl, *lent)


CONV_TT = 256
CONV_RC = 32
_CBLK = 1536


def _conv_tail(y, g, ln_g_ref, ln_b_ref, wpw_ref, bpw_ref):
    mu = jnp.mean(y, axis=-1, keepdims=True)
    var = jnp.mean(jnp.square(y - mu), axis=-1, keepdims=True)
    yn = (y - mu) * lax.rsqrt(var + EPS) * ln_g_ref[...] + ln_b_ref[...]
    pw = jnp.dot(_silu(yn).astype(BF16), wpw_ref[...], preferred_element_type=F32) + bpw_ref[...]
    return pw * _silu(g)


def _conv_prefill_body(z7_ref, z8_ref, wdw_ref, bdw_ref, ln_g_ref, ln_b_ref, wpw_ref, bpw_ref,
                       *rest, n_lent):
    o_ref, s_ref, ext, shifted, ybuf = rest[n_lent:]
    t = pl.program_id(1)
    nt = pl.num_programs(1)
    tt = CONV_TT
    sub = 8

    @pl.when(t == 0)
    def _():
        ext[0:HIST, :] = jnp.zeros((HIST, CONV_W), F32)

    a = z7_ref[:, 0:CONV_W]
    bgate = jnp.concatenate([z7_ref[:, CONV_W:_CBLK], z8_ref[:, 0:2 * CONV_W - _CBLK]], axis=-1)
    g = z8_ref[:, 2 * CONV_W - _CBLK:]
    ext[HIST:HIST + tt, :] = a * _sigmoid(bgate)

    base = HIST - (CONV_K - 1)
    n_sh = HIST + tt - sub
    for r in range(1, sub):
        shifted[r, 0:n_sh, :] = ext[r:r + n_sh, :]
    for ci in range(tt // CONV_RC):
        r0 = ci * CONV_RC
        acc = jnp.zeros((CONV_RC // sub, sub, CONV_W), F32)
        for k in range(CONV_K):
            r = (base + k) % sub
            a0 = r0 + (base + k) - r
            rows = ext[a0:a0 + CONV_RC, :] if r == 0 else shifted[r, a0:a0 + CONV_RC, :]
            acc = acc + wdw_ref[k][None] * rows.reshape(CONV_RC // sub, sub, CONV_W)
        ybuf[r0:r0 + CONV_RC, :] = acc.reshape(CONV_RC, CONV_W)
    y = ybuf[...] + bdw_ref[...]
    o_ref[...] = _conv_tail(y, g, ln_g_ref, ln_b_ref, wpw_ref, bpw_ref).astype(o_ref.dtype)

    @pl.when(t == nt - 1)
    def _():
        s_ref[0] = ext[tt + HIST - (CONV_K - 1):tt + HIST, :]

    ext[0:HIST, :] = ext[tt:tt + HIST, :]


def _conv_prefill(z, mix, w_dw, b_dw, ln_g, ln_b, wpw_all, b_pw, batch, seq, layer, s_all):
    tt = CONV_TT
    nt = seq // tt
    vec = lambda i, t: (0, 0)
    in_specs = [pl.BlockSpec((tt, _CBLK), lambda b, t: (b * nt + t, OFF_AC // _CBLK)),
                pl.BlockSpec((tt, _CBLK), lambda b, t: (b * nt + t, OFF_AC // _CBLK + 1)),
                pl.BlockSpec((CONV_K, 8, CONV_W), lambda b, t: (0, 0, 0)),
                pl.BlockSpec((1, CONV_W), vec),
                pl.BlockSpec((1, CONV_W), vec),
                pl.BlockSpec((1, CONV_W), vec),
                pl.BlockSpec((None, CONV_W, CONV_W), lambda b, t: (layer, 0, 0)),
                pl.BlockSpec((1, CONV_W), vec)]
    lent_specs, aliases, lent = _lend(len(in_specs), [mix, s_all], [0, 1])
    return pl.pallas_call(
        functools.partial(_conv_prefill_body, n_lent=len(lent)),
        grid=(batch, nt),
        in_specs=in_specs + lent_specs,
        out_specs=[pl.BlockSpec((tt, CONV_W), lambda b, t: (b * nt + t, (RET_W + HG_W) // CONV_W)),
                   pl.BlockSpec((None, 1, CONV_K - 1, CONV_W), lambda b, t: (layer, b, 0, 0))],
        out_shape=[jax.ShapeDtypeStruct(mix.shape, BF16),
                   jax.ShapeDtypeStruct((DEPTH, batch, CONV_K - 1, CONV_W), F32)],
        scratch_shapes=[pltpu.VMEM((HIST + tt, CONV_W), F32), pltpu.VMEM((8, HIST + tt, CONV_W), F32),
                        pltpu.VMEM((tt, CONV_W), F32)],
        input_output_aliases=aliases,
        compiler_params=_params(("parallel", "arbitrary")),
        name="conv_prefill",
    )(z, z, jnp.broadcast_to(w_dw[:, None, :], (CONV_K, 8, CONV_W)), b_dw.reshape(1, -1),
      ln_g.reshape(1, -1), ln_b.reshape(1, -1), wpw_all, b_pw.reshape(1, -1), *lent)


def _ret_decode_body(qT_ref, kT_ref, v_ref, g_ref, cos_ref, sin_ref, lg_ref, gn_ref, s_ref, *rest,
                     n_lent):
    o_ref, so_ref, orow = rest[n_lent:]
    half = RET_DK // 2
    cosc = cos_ref[...]
    sinc = sin_ref[...]
    for h in range(RET_HEADS):
        zq = qT_ref[h]
        zk = kT_ref[h]
        q = zq * cosc + jnp.concatenate([zq[half:], zq[:half]], axis=0) * sinc
        k = (zk * cosc + jnp.concatenate([zk[half:], zk[:half]], axis=0) * sinc) * (RET_DK ** -0.5)
        gamma = jnp.exp(lg_ref[h:h + 1, :])
        for j in range(DEC_BB):
            vrow = v_ref[j:j + 1, h * RET_DV:(h + 1) * RET_DV]
            s1 = gamma * s_ref[j, h] + k[:, j:j + 1] * vrow
            so_ref[j, h] = s1
            orow[j:j + 1, h * RET_DV:(h + 1) * RET_DV] = jnp.sum(q[:, j:j + 1] * s1, axis=0, keepdims=True)
    for h in range(RET_HEADS):
        sl = slice(h * RET_DV, (h + 1) * RET_DV)
        o = orow[:, sl]
        mu = jnp.mean(o, axis=-1, keepdims=True)
        var = jnp.mean(jnp.square(o - mu), axis=-1, keepdims=True)
        on = (o - mu) * lax.rsqrt(var + EPS) * gn_ref[:, sl]
        o_ref[:, sl] = on * _silu(g_ref[:, sl])


def _cols(zs, off, heads, dk):
    bsz = zs.shape[0]
    x = zs[:, off:off + heads * dk].reshape(bsz // DEC_BB, DEC_BB, heads, dk)
    return x.transpose(0, 2, 3, 1)


def _ret_decode(zs, state_all, layer, cosc, sinc, lg_t, gn_g, so_all):
    bsz = zs.shape[0]
    bb = DEC_BB
    qT = _cols(zs, OFF_QR, RET_HEADS, RET_DK)
    kT = _cols(zs, OFF_KR, RET_HEADS, RET_DK)
    col_spec = pl.BlockSpec((None, RET_HEADS, RET_DK, bb), lambda i: (i, 0, 0, 0))
    st_spec = pl.BlockSpec((None, bb, RET_HEADS, RET_DK, RET_DV), lambda i: (layer, i, 0, 0, 0))
    in_specs = [col_spec, col_spec,
                pl.BlockSpec((bb, RET_W), lambda i: (i, OFF_VR // RET_W)),
                pl.BlockSpec((bb, RET_W), lambda i: (i, OFF_GR // RET_W)),
                pl.BlockSpec((RET_DK, bb), lambda i: (0, 0)),
                pl.BlockSpec((RET_DK, bb), lambda i: (0, 0)),
                pl.BlockSpec((RET_HEADS, RET_DV), lambda i: (0, 0)),
                pl.BlockSpec((1, RET_W), lambda i: (0, 0)),
                st_spec]
    lent_specs, aliases, lent = _lend(len(in_specs), [so_all], [1])
    return pl.pallas_call(
        functools.partial(_ret_decode_body, n_lent=len(lent)),
        grid=(bsz // bb,),
        in_specs=in_specs + lent_specs,
        out_specs=[pl.BlockSpec((bb, RET_W), lambda i: (i, 0)), st_spec],
        out_shape=[jax.ShapeDtypeStruct((bsz, RET_W), F32),
                   jax.ShapeDtypeStruct(state_all.shape, F32)],
        scratch_shapes=[pltpu.VMEM((bb, RET_W), F32)],
        input_output_aliases=aliases,
        compiler_params=_params(("parallel",)),
        name="ret_decode",
    )(qT, kT, zs, zs, cosc, sinc, lg_t, gn_g.reshape(1, RET_W), state_all, *lent)


def _hgrn_decode_body(q_ref, fT_ref, v_ref, g_ref, lbT_ref, ng_ref, s_ref, *rest, n_lent):
    o_ref, so_ref, orow = rest[n_lent:]
    for h in range(HG_HEADS):
        sl = slice(h * HG_DV, (h + 1) * HG_DV)
        qb = _silu(q_ref[:, sl]).astype(BF16)
        lb = lbT_ref[h]
        f = lb + (1.0 - lb) * _sigmoid(fT_ref[h])
        for j in range(DEC_BB):
            vrow = v_ref[j:j + 1, sl]
            s1 = f[:, j:j + 1] * (s_ref[j, h] - vrow) + vrow
            so_ref[j, h] = s1
            oj = jnp.dot(qb, s1.astype(BF16), preferred_element_type=F32)
            orow[j:j + 1, sl] = oj[j:j + 1, :]
    for h in range(HG_HEADS):
        sl = slice(h * HG_DV, (h + 1) * HG_DV)
        o = orow[:, sl]
        on = o * lax.rsqrt(jnp.mean(o * o, axis=-1, keepdims=True) + EPS) * ng_ref[:, sl]
        o_ref[:, sl] = on * _silu(g_ref[:, sl])


def _hgrn_decode(zs, state_all, layer, lb, ng, so_all):
    bsz = zs.shape[0]
    bb = DEC_BB
    fT = _cols(zs, OFF_FH, HG_HEADS, HG_DK)
    col_spec = pl.BlockSpec((None, HG_HEADS, HG_DK, bb), lambda i: (i, 0, 0, 0))
    st_spec = pl.BlockSpec((None, bb, HG_HEADS, HG_DK, HG_DV), lambda i: (layer, i, 0, 0, 0))
    in_specs = [pl.BlockSpec((bb, HG_W), lambda i: (i, OFF_QH // HG_W)),
                col_spec,
                pl.BlockSpec((bb, HG_W), lambda i: (i, OFF_IH // HG_W)),
                pl.BlockSpec((bb, HG_W), lambda i: (i, OFF_GH // HG_W)),
                pl.BlockSpec((HG_HEADS, HG_DK, 1), lambda i: (0, 0, 0)),
                pl.BlockSpec((1, HG_W), lambda i: (0, 0)),
                st_spec]
    lent_specs, aliases, lent = _lend(len(in_specs), [so_all], [1])
    return pl.pallas_call(
        functools.partial(_hgrn_decode_body, n_lent=len(lent)),
        grid=(bsz // bb,),
        in_specs=in_specs + lent_specs,
        out_specs=[pl.BlockSpec((bb, HG_W), lambda i: (i, 0)), st_spec],
        out_shape=[jax.ShapeDtypeStruct((bsz, HG_W), F32),
                   jax.ShapeDtypeStruct(state_all.shape, F32)],
        scratch_shapes=[pltpu.VMEM((bb, HG_W), F32)],
        input_output_aliases=aliases,
        compiler_params=_params(("parallel",)),
        name="hgrn_decode",
    )(zs, fT, zs, zs, lb.reshape(HG_HEADS, HG_DK, 1), ng.reshape(1, HG_W), state_all, *lent)


def _conv_decode_body(z7_ref, z8_ref, wdw_ref, bdw_ref, ln_g_ref, ln_b_ref, wpw_ref, bpw_ref, s_ref,
                      *rest, n_lent):
    o_ref, so_ref = rest[n_lent:]
    kh = CONV_K - 1
    a = z7_ref[:, 0:CONV_W]
    bgate = jnp.concatenate([z7_ref[:, CONV_W:_CBLK], z8_ref[:, 0:2 * CONV_W - _CBLK]], axis=-1)
    g = z8_ref[:, 2 * CONV_W - _CBLK:]
    u = a * _sigmoid(bgate)
    buf = s_ref[...]
    y = jnp.sum(buf * wdw_ref[0:kh, :][None], axis=1) + u * wdw_ref[kh:kh + 1, :] + bdw_ref[...]
    o_ref[...] = _conv_tail(y, g, ln_g_ref, ln_b_ref, wpw_ref, bpw_ref)
    so_ref[:, 0:kh - 1, :] = s_ref[:, 1:kh, :]
    for j in range(DEC_BB):
        so_ref[j, kh - 1:kh, :] = u[j:j + 1, :]


def _conv_decode(zs, state_all, layer, w_dw, b_dw, ln_g, ln_b, wpw_all, b_pw, so_all):
    bsz = zs.shape[0]
    bb = DEC_BB
    vec = lambda i: (0, 0)
    st_spec = pl.BlockSpec((None, bb, CONV_K - 1, CONV_W), lambda i: (layer, i, 0, 0))
    in_specs = [pl.BlockSpec((bb, _CBLK), lambda i: (i, OFF_AC // _CBLK)),
                pl.BlockSpec((bb, _CBLK), lambda i: (i, OFF_AC // _CBLK + 1)),
                pl.BlockSpec((CONV_K, CONV_W), vec),
                pl.BlockSpec((1, CONV_W), vec),
                pl.BlockSpec((1, CONV_W), vec),
                pl.BlockSpec((1, CONV_W), vec),
                pl.BlockSpec((None, CONV_W, CONV_W), lambda i: (layer, 0, 0)),
                pl.BlockSpec((1, CONV_W), vec),
                st_spec]
    lent_specs, aliases, lent = _lend(len(in_specs), [so_all], [1])
    return pl.pallas_call(
        functools.partial(_conv_decode_body, n_lent=len(lent)),
        grid=(bsz // bb,),
        in_specs=in_specs + lent_specs,
        out_specs=[pl.BlockSpec((bb, CONV_W), lambda i: (i, 0)), st_spec],
        out_shape=[jax.ShapeDtypeStruct((bsz, CONV_W), F32),
                   jax.ShapeDtypeStruct(state_all.shape, F32)],
        input_output_aliases=aliases,
        compiler_params=_params(("parallel",)),
        name="conv_decode",
    )(zs, zs, w_dw, b_dw.reshape(1, -1), ln_g.reshape(1, -1), ln_b.reshape(1, -1), wpw_all,
      b_pw.reshape(1, -1), state_all, *lent)


def _rope_tables(pos):
    half = RET_DK // 2
    inv = ROPE_BASE ** (-jnp.arange(half, dtype=F32) / half)
    ang = pos[:, None] * inv[None, :]
    cos = jnp.cos(ang)
    sin = jnp.sin(ang)
    return jnp.concatenate([cos, cos], axis=-1), jnp.concatenate([-sin, sin], axis=-1)


PROMPT_TILES_IN = (1024, 1536, 2048)
PROMPT_TILES_SQ = (1024, 512, D_MODEL)
SAMPLE_TN_IN = 768
SAMPLE_TN_SQ = 512


def kernel(x_prompt, x_sample, state_ret, state_hgrn, state_conv, p_prompt, p_sample, norm_g, w_in,
           ret_gn_g, hg_lower_bounds, hg_norm_g, w_dw, b_dw, conv_ln_g, conv_ln_b, w_pw, b_pw, w_out,
           w_ple, w_pg, final_norm_g):
    bp, tp, d = x_prompt.shape
    bs, ts, _ = x_sample.shape
    mp, ms = bp * tp, bs * ts
    assert ts == 1 and tp % CONV_TT == 0 and bs % DEC_BB == 0

    w_ple_b = w_ple.astype(BF16)
    w_pw_b = w_pw.astype(BF16)
    pp_b = p_prompt.reshape(DEPTH, mp, PLE_DIM).astype(BF16)
    ps_b = p_sample.reshape(DEPTH, ms, PLE_DIM).astype(BF16)

    lbs = jax.nn.softmax(hg_lower_bounds.astype(F32), axis=0)
    lbs = jnp.cumsum(lbs, axis=0) - lbs[0:1]
    log_gamma = np.log(1.0 - 2.0 ** (-5.0 - np.arange(RET_HEADS, dtype=np.float32))).astype(np.float32)
    lg_t = jnp.asarray(np.broadcast_to(log_gamma[:, None], (RET_HEADS, RET_DV)))

    cos_p, sin_p = _rope_tables(jnp.arange(tp, dtype=F32))
    cos_s, sin_s = _rope_tables(PAST_LEN + jnp.arange(ts, dtype=F32))
    cosc = jnp.broadcast_to(cos_s.reshape(RET_DK, 1), (RET_DK, DEC_BB))
    sinc = jnp.broadcast_to(sin_s.reshape(RET_DK, 1), (RET_DK, DEC_BB))

    def sample_layer(h, l, states):
        s_r, s_h, s_c = states
        xn = _rmsnorm(h, norm_g[l], BF16, ms)
        z, w_in_b = _mm(xn, w_in, l, mode="plain", tm=ms, tn=SAMPLE_TN_IN, tk=d)
        mix_r, s_r = _ret_decode(z, state_ret, l, cosc, sinc, lg_t, ret_gn_g[l], s_r)
        mix_h, s_h = _hgrn_decode(z, state_hgrn, l, lbs[l], hg_norm_g[l], s_h)
        mix_c, s_c = _conv_decode(z, state_conv, l, w_dw[l], b_dw[l], conv_ln_g[l], conv_ln_b[l],
                                  w_pw_b, b_pw[l], s_c)
        mix = jnp.concatenate([mix_r, mix_h, mix_c], axis=-1).astype(BF16)
        h1, h1b, w_out_b = _mm(mix, w_out, l, mode="resid", tm=ms, tn=SAMPLE_TN_SQ, tk=d, resid=h)
        h2, w_pg_b = _mm(h1b, w_pg, l, mode="ple", tm=ms, tn=SAMPLE_TN_SQ, tk=d, resid=h1, p=ps_b,
                         wp_all=w_ple_b)
        return h2, (s_r, s_h, s_c), (w_in_b, w_out_b, w_pg_b)

    def prompt_layer(h, l, states, weights_b):
        s_r, s_h, s_c = states
        w_in_b, w_out_b, w_pg_b = weights_b
        tm, tn, tk = PROMPT_TILES_IN
        xn = _rmsnorm(h, norm_g[l], BF16, 256)
        (z,) = _mm(xn, w_in_b, l, mode="plain", tm=tm, tn=tn, tk=tk)
        mix, s_r = _ret_prefill(z, cos_p, sin_p, lg_t, ret_gn_g[l], bp, tp, l, s_r)
        mix, s_h = _hgrn_prefill(z, mix, lbs[l], hg_norm_g[l], bp, tp, l, s_h)
        mix, s_c = _conv_prefill(z, mix, w_dw[l], b_dw[l], conv_ln_g[l], conv_ln_b[l], w_pw_b,
                                 b_pw[l], bp, tp, l, s_c)
        tm, tn, tk = PROMPT_TILES_SQ
        h1, h1b = _mm(mix, w_out_b, l, mode="resid", tm=tm, tn=tn, tk=tk, resid=h)
        (h2,) = _mm(h1b, w_pg_b, l, mode="ple", tm=tm, tn=tn, tk=tk, resid=h1, p=pp_b, wp_all=w_ple_b)
        return h2, (s_r, s_h, s_c)

    hp, hs = x_prompt.reshape(mp, d), x_sample.reshape(ms, d)
    st_p = st_s = (None, None, None)
    for l in range(DEPTH):
        hs, st_s, weights_b = sample_layer(hs, l, st_s)
        hp, st_p = prompt_layer(hp, l, st_p, weights_b)
    y_p = _rmsnorm(hp, final_norm_g, F32, 256).reshape(x_prompt.shape)
    y_s = _rmsnorm(hs, final_norm_g, F32, ms).reshape(x_sample.shape)
    return (y_p, y_s) + st_p + st_s
```

```python
import functools
import math

import numpy as np
import jax
import jax.numpy as jnp
from jax import lax
from jax.experimental import pallas as pl
from jax.experimental.pallas import tpu as pltpu

F32 = jnp.float32
BF16 = jnp.bfloat16

D_MODEL = 4096
DEPTH = 4
PAST_LEN = 16384
PLE_DIM = 256
RET_HEADS = 6
RET_DK = 128
RET_DV = 256
RET_W = RET_HEADS * RET_DV
ROPE_BASE = 10000.0
HG_HEADS = 12
HG_DK = 128
HG_DV = 128
HG_W = HG_HEADS * HG_DV
CONV_W = D_MODEL - RET_W - HG_W
CONV_K = 31
EPS = 1e-6
IN_W = 2 * RET_HEADS * RET_DK + 2 * RET_W + 2 * HG_HEADS * HG_DK + 2 * HG_W + 3 * CONV_W

OFF_QR = 0
OFF_KR = OFF_QR + RET_HEADS * RET_DK
OFF_VR = OFF_KR + RET_HEADS * RET_DK
OFF_GR = OFF_VR + RET_W
OFF_QH = OFF_GR + RET_W
OFF_FH = OFF_QH + HG_HEADS * HG_DK
OFF_IH = OFF_FH + HG_HEADS * HG_DK
OFF_GH = OFF_IH + HG_W
OFF_AC = OFF_GH + HG_W
OFF_BC = OFF_AC + CONV_W
OFF_GC = OFF_BC + CONV_W

V7X_VMEM_LIMIT_BYTES = 56 * 1024 * 1024
CHUNK = 128
HIST = 32
DEC_BB = 8

_NT = (((1,), (1,)), ((), ()))
_ANY = pl.BlockSpec(memory_space=pl.ANY)


def _params(sem):
    return pltpu.CompilerParams(dimension_semantics=sem, vmem_limit_bytes=V7X_VMEM_LIMIT_BYTES)


def _sigmoid(x):
    return 0.5 * jnp.tanh(0.5 * x) + 0.5


def _silu(x):
    return x * _sigmoid(x)


def _lend(n_in, carried, out_positions):
    specs, aliases = [], {}
    for buf, out_pos in zip(carried, out_positions):
        if buf is not None:
            aliases[n_in + len(specs)] = out_pos
            specs.append(_ANY)
    return specs, aliases, [b for b in carried if b is not None]


def _rmsnorm_body(x_ref, g_ref, o_ref):
    x = x_ref[...]
    ms = jnp.mean(x * x, axis=-1, keepdims=True)
    o_ref[...] = (x * lax.rsqrt(ms + EPS) * g_ref[...]).astype(o_ref.dtype)


def _rmsnorm(x, g, out_dtype, tm):
    m, d = x.shape
    return pl.pallas_call(
        _rmsnorm_body,
        grid=(m // tm,),
        in_specs=[pl.BlockSpec((tm, d), lambda i: (i, 0)),
                  pl.BlockSpec((1, d), lambda i: (0, 0))],
        out_specs=pl.BlockSpec((tm, d), lambda i: (i, 0)),
        out_shape=jax.ShapeDtypeStruct((m, d), out_dtype),
        compiler_params=_params(("parallel",)),
        name="rmsnorm",
    )(x, g.reshape(1, d))


def _mm_body(x_ref, w_ref, *rest, nk, mode, cast_w):
    if nk > 1:
        acc_ref = rest[-1]
        rest = rest[:-1]
    if cast_w:
        wb_ref = rest[-1]
        rest = rest[:-1]

    def partial_product():
        w = w_ref[...]
        if cast_w:
            w = w.astype(BF16)
            wb_ref[...] = w
        return jnp.dot(x_ref[...], w, preferred_element_type=F32)

    def epilogue(acc):
        if mode == "plain":
            (o_ref,) = rest
            o_ref[...] = acc
        elif mode == "resid":
            r_ref, o_ref, ob_ref = rest
            h1 = r_ref[...] + acc
            o_ref[...] = h1
            ob_ref[...] = h1.astype(BF16)
        else:
            r_ref, p_ref, wp_ref, o_ref = rest
            ple = jnp.dot(p_ref[...], wp_ref[...], preferred_element_type=F32)
            o_ref[...] = r_ref[...] + _sigmoid(acc) * ple

    if nk == 1:
        epilogue(partial_product())
        return
    k = pl.program_id(2)

    @pl.when(k == 0)
    def _():
        acc_ref[...] = partial_product()

    if nk > 2:
        @pl.when(jnp.logical_and(k > 0, k < nk - 1))
        def _():
            acc_ref[...] += partial_product()

    @pl.when(k == nk - 1)
    def _():
        epilogue(acc_ref[...] + partial_product())


def _mm(x, w, layer, *, mode, tm, tn, tk, resid=None, p=None, wp_all=None):
    m, kdim = x.shape
    n = w.shape[-1]
    nk = kdim // tk
    cast_w = w.ndim == 3
    assert not cast_w or (nk == 1 and m == tm)
    grid = (m // tm, n // tn, nk)
    w_spec = (pl.BlockSpec((None, tk, tn), lambda i, j, k: (layer, k, j)) if cast_w
              else pl.BlockSpec((tk, tn), lambda i, j, k: (k, j)))
    in_specs = [pl.BlockSpec((tm, tk), lambda i, j, k: (i, k)), w_spec]
    args = [x, w]
    o_spec = pl.BlockSpec((tm, tn), lambda i, j, k: (i, j))
    out_specs = [o_spec]
    out_shape = [jax.ShapeDtypeStruct((m, n), F32)]
    if mode == "resid":
        in_specs.append(o_spec)
        args.append(resid)
        out_specs.append(o_spec)
        out_shape.append(jax.ShapeDtypeStruct((m, n), BF16))
    elif mode == "ple":
        pdim = p.shape[-1]
        in_specs += [o_spec,
                     pl.BlockSpec((None, tm, pdim), lambda i, j, k: (layer, i, 0)),
                     pl.BlockSpec((None, pdim, tn), lambda i, j, k: (layer, 0, j))]
        args += [resid, p, wp_all]
    if cast_w:
        out_specs.append(pl.BlockSpec((tk, tn), lambda i, j, k: (k, j)))
        out_shape.append(jax.ShapeDtypeStruct((kdim, n), BF16))
    return pl.pallas_call(
        functools.partial(_mm_body, nk=nk, mode=mode, cast_w=cast_w),
        grid=grid,
        in_specs=in_specs,
        out_specs=out_specs,
        out_shape=out_shape,
        scratch_shapes=[pltpu.VMEM((tm, tn), F32)] if nk > 1 else [],
        compiler_params=_params(("parallel", "parallel", "arbitrary")),
        name="mm_" + mode + ("_cast" if cast_w else ""),
    )(*args)


def _ret_prefill_body(q_ref, k_ref, v_ref, g_ref, cos_ref, sin_ref, lg_ref, gn_ref, *rest, n_lent):
    o_ref, s_ref, st_scr = rest[n_lent:]
    c = pl.program_id(1)
    nc = pl.num_programs(1)
    L = CHUNK

    @pl.when(c == 0)
    def _():
        st_scr[...] = jnp.zeros_like(st_scr)

    cos = cos_ref[...]
    sin = sin_ref[...]
    row = lax.broadcasted_iota(jnp.int32, (L, L), 0)
    col = lax.broadcasted_iota(jnp.int32, (L, L), 1)
    diff = (row - col).astype(F32)
    causal = row >= col
    rowf = lax.broadcasted_iota(jnp.int32, (L, RET_DK), 0).astype(F32)
    for h in range(RET_HEADS):
        lg = lg_ref[h:h + 1, :]
        lg1 = lg[:, :RET_DK]
        decay_in = jnp.where(causal, jnp.exp(jnp.where(causal, diff, 0.0) * lg1), 0.0)
        q_decay = jnp.exp((rowf + 1.0) * lg1)
        k_decay = jnp.exp((L - 1.0 - rowf) * lg1)
        chunk_decay = jnp.exp(L * lg)

        zq = q_ref[:, h * RET_DK:(h + 1) * RET_DK]
        zk = k_ref[:, h * RET_DK:(h + 1) * RET_DK]
        v = v_ref[:, h * RET_DV:(h + 1) * RET_DV]
        zg = g_ref[:, h * RET_DV:(h + 1) * RET_DV]
        q = zq * cos + pltpu.roll(zq, RET_DK // 2, 1) * sin
        k = (zk * cos + pltpu.roll(zk, RET_DK // 2, 1) * sin) * (RET_DK ** -0.5)
        vb = v.astype(BF16)
        s0 = st_scr[h]

        scores = lax.dot_general(q.astype(BF16), k.astype(BF16), _NT,
                                 preferred_element_type=F32) * decay_in
        o = (jnp.dot(scores.astype(BF16), vb, preferred_element_type=F32)
             + jnp.dot((q * q_decay).astype(BF16), s0.astype(BF16), preferred_element_type=F32))
        kT = (k * k_decay).T.astype(BF16)
        st_scr[h] = s0 * chunk_decay + jnp.dot(kT, vb, preferred_element_type=F32)

        mu = jnp.mean(o, axis=-1, keepdims=True)
        var = jnp.mean(jnp.square(o - mu), axis=-1, keepdims=True)
        on = (o - mu) * lax.rsqrt(var + EPS) * gn_ref[:, h * RET_DV:(h + 1) * RET_DV]
        o_ref[:, h * RET_DV:(h + 1) * RET_DV] = (on * _silu(zg)).astype(o_ref.dtype)

    @pl.when(c == nc - 1)
    def _():
        s_ref[0] = st_scr[...]


def _ret_prefill(z, cos_t, sin_t, lg_t, gn_g, batch, seq, layer, s_all):
    m = z.shape[0]
    nc = seq // CHUNK
    L = CHUNK
    rowmap = lambda blk: (lambda b, c: (b * nc + c, blk))
    in_specs = [pl.BlockSpec((L, RET_HEADS * RET_DK), rowmap(OFF_QR // (RET_HEADS * RET_DK))),
                pl.BlockSpec((L, RET_HEADS * RET_DK), rowmap(OFF_KR // (RET_HEADS * RET_DK))),
                pl.BlockSpec((L, RET_W), rowmap(OFF_VR // RET_W)),
                pl.BlockSpec((L, RET_W), rowmap(OFF_GR // RET_W)),
                pl.BlockSpec((L, RET_DK), lambda b, c: (c, 0)),
                pl.BlockSpec((L, RET_DK), lambda b, c: (c, 0)),
                pl.BlockSpec((RET_HEADS, RET_DV), lambda b, c: (0, 0)),
                pl.BlockSpec((1, RET_W), lambda b, c: (0, 0))]
    lent_specs, aliases, lent = _lend(len(in_specs), [s_all], [1])
    return pl.pallas_call(
        functools.partial(_ret_prefill_body, n_lent=len(lent)),
        grid=(batch, nc),
        in_specs=in_specs + lent_specs,
        out_specs=[pl.BlockSpec((L, RET_W), lambda b, c: (b * nc + c, 0)),
                   pl.BlockSpec((None, 1, RET_HEADS, RET_DK, RET_DV), lambda b, c: (layer, b, 0, 0, 0))],
        out_shape=[jax.ShapeDtypeStruct((m, D_MODEL), BF16),
                   jax.ShapeDtypeStruct((DEPTH, batch, RET_HEADS, RET_DK, RET_DV), F32)],
        scratch_shapes=[pltpu.VMEM((RET_HEADS, RET_DK, RET_DV), F32)],
        input_output_aliases=aliases,
        compiler_params=_params(("parallel", "arbitrary")),
        name="ret_prefill",
    )(z, z, z, z, cos_t, sin_t, lg_t, gn_g.reshape(1, RET_W), *lent)


HG_HB = 12
_NLEV = int(math.log2(CHUNK))


def _level_table():
    i = np.arange(CHUNK)[:, None]
    j = np.arange(CHUNK)[None, :]
    x = np.bitwise_xor(i, j)
    lvl = np.floor(np.log2(np.maximum(x, 1))).astype(np.int32)
    lvl = np.where(i == j, _NLEV, lvl)
    lvl = np.where(i < j, -1, lvl)
    return lvl.astype(np.int32)


def _split3(x):
    x1 = x.astype(BF16)
    r1 = x - x1.astype(F32)
    x2 = r1.astype(BF16)
    x3 = (r1 - x2.astype(F32)).astype(BF16)
    return x1, x2, x3


def _ref_rows(b, m):
    L, w = b.shape
    g = 2 * m
    b3 = b.reshape(L // g, g, w)
    r = jnp.broadcast_to(b3[:, m - 1:m, :], b3.shape)
    return r.reshape(L, w)


def _hgrn_prefill_body(q_ref, f_ref, i_ref, g_ref, lb_ref, ng_ref, tri_ref, lvl_ref, *rest, n_lent):
    o_ref, s_ref, st_scr = rest[n_lent:]
    c = pl.program_id(2)
    nc = pl.num_programs(2)
    L = CHUNK

    @pl.when(c == 0)
    def _():
        st_scr[...] = jnp.zeros_like(st_scr)

    lvl = lvl_ref[...]
    rowi = lax.broadcasted_iota(jnp.int32, (L, HG_DK), 0)
    odd = (rowi & 1) == 1
    r4 = rowi & 3
    r4_is0, r4_is1, r4_is2 = r4 == 0, r4 == 1, r4 == 2

    lb_all = lb_ref[...]
    f_all = lb_all + (1.0 - lb_all) * _sigmoid(f_ref[...])
    l1, l2, l3 = _split3(jnp.log(f_all))
    b_parts = jnp.dot(tri_ref[...], jnp.concatenate([l1, l2, l3], axis=1), preferred_element_type=F32)
    wblk = HG_HB * HG_DK
    b_all = b_parts[:, 0:wblk] + b_parts[:, wblk:2 * wblk] + b_parts[:, 2 * wblk:]
    for hh in range(HG_HB):
        sl = slice(hh * HG_DK, (hh + 1) * HG_DK)
        v = i_ref[:, sl]
        zg = g_ref[:, sl]
        q = _silu(q_ref[:, sl])
        f = f_all[:, sl]
        k = 1.0 - f
        b = b_all[:, sl]
        qb = q.astype(BF16)
        kb = k.astype(BF16)
        vb = v.astype(BF16)

        a = jnp.where(lvl == _NLEV, lax.dot_general(qb, kb, _NT, preferred_element_type=F32), 0.0)
        f_prev = pltpu.roll(f, 1, 0)
        f_next = pltpu.roll(f, L - 1, 0)
        for t in range(_NLEV):
            m = 1 << t
            if m == 1:
                e = jnp.where(odd, f, 1.0)
            elif m == 2:
                e = jnp.where(r4_is0, f_next, jnp.where(r4_is1, 1.0, jnp.where(r4_is2, f, f * f_prev)))
            else:
                e = jnp.exp(-jnp.abs(b - _ref_rows(b, m)))
            eb = e.astype(BF16)
            p = lax.dot_general(qb * eb, kb * eb, _NT, preferred_element_type=F32)
            a = jnp.where(lvl == t, p, a)

        st = st_scr[hh]
        o = (jnp.dot(a.astype(BF16), vb, preferred_element_type=F32)
             + lax.dot_general((q * jnp.exp(b)).astype(BF16), st.astype(BF16), _NT,
                               preferred_element_type=F32))
        b_last = b[L - 1:L, :]
        k_s = (k * jnp.exp(b_last - b)).astype(BF16)
        st_scr[hh] = st * jnp.exp(b_last) + jnp.dot(v.T.astype(BF16), k_s, preferred_element_type=F32)

        on = o * lax.rsqrt(jnp.mean(o * o, axis=-1, keepdims=True) + EPS) * ng_ref[:, sl]
        o_ref[:, sl] = (on * _silu(zg)).astype(o_ref.dtype)

    @pl.when(c == nc - 1)
    def _():
        for hh in range(HG_HB):
            s_ref[0, hh] = st_scr[hh].T


def _hgrn_prefill(z, mix, lb, ng, batch, seq, layer, s_all):
    nc = seq // CHUNK
    L = CHUNK
    ng_groups = HG_HEADS // HG_HB
    wblk = HG_HB * HG_DK
    rowmap = lambda off: (lambda b, g, c: (b * nc + c, off // wblk + g))
    tri = jnp.asarray(np.tril(np.ones((L, L), np.float32)), dtype=BF16)
    lvl = jnp.asarray(_level_table())
    in_specs = [pl.BlockSpec((L, wblk), rowmap(OFF_QH)),
                pl.BlockSpec((L, wblk), rowmap(OFF_FH)),
                pl.BlockSpec((L, wblk), rowmap(OFF_IH)),
                pl.BlockSpec((L, wblk), rowmap(OFF_GH)),
                pl.BlockSpec((1, wblk), lambda b, g, c: (0, g)),
                pl.BlockSpec((1, wblk), lambda b, g, c: (0, g)),
                pl.BlockSpec((L, L), lambda b, g, c: (0, 0)),
                pl.BlockSpec((L, L), lambda b, g, c: (0, 0))]
    lent_specs, aliases, lent = _lend(len(in_specs), [mix, s_all], [0, 1])
    return pl.pallas_call(
        functools.partial(_hgrn_prefill_body, n_lent=len(lent)),
        grid=(batch, ng_groups, nc),
        in_specs=in_specs + lent_specs,
        out_specs=[pl.BlockSpec((L, wblk), rowmap(RET_W)),
                   pl.BlockSpec((None, 1, HG_HB, HG_DK, HG_DV), lambda b, g, c: (layer, b, g, 0, 0))],
        out_shape=[jax.ShapeDtypeStruct(mix.shape, BF16),
                   jax.ShapeDtypeStruct((DEPTH, batch, HG_HEADS, HG_DK, HG_DV), F32)],
        scratch_shapes=[pltpu.VMEM((HG_HB, HG_DV, HG_DK), F32)],
        input_output_aliases=aliases,
        compiler_params=_params(("parallel", "parallel", "arbitrary")),
        name="hgrn_prefill",
    )(z, z, z, z, lb.reshape(1, HG_W), ng.reshape(1, HG_W), tri, lvl, *lent)


CONV_TT = 256
CONV_RC = 32
_CBLK = 1536


def _conv_tail(y, g, ln_g_ref, ln_b_ref, wpw_ref, bpw_ref):
    mu = jnp.mean(y, axis=-1, keepdims=True)
    var = jnp.mean(jnp.square(y - mu), axis=-1, keepdims=True)
    yn = (y - mu) * lax.rsqrt(var + EPS) * ln_g_ref[...] + ln_b_ref[...]
    pw = jnp.dot(_silu(yn).astype(BF16), wpw_ref[...], preferred_element_type=F32) + bpw_ref[...]
    return pw * _silu(g)


def _conv_prefill_body(z7_ref, z8_ref, wdw_ref, bdw_ref, ln_g_ref, ln_b_ref, wpw_ref, bpw_ref,
                       *rest, n_lent):
    o_ref, s_ref, ext, shifted, ybuf = rest[n_lent:]
    t = pl.program_id(1)
    nt = pl.num_programs(1)
    tt = CONV_TT
    sub = 8

    @pl.when(t == 0)
    def _():
        ext[0:HIST, :] = jnp.zeros((HIST, CONV_W), F32)

    a = z7_ref[:, 0:CONV_W]
    bgate = jnp.concatenate([z7_ref[:, CONV_W:_CBLK], z8_ref[:, 0:2 * CONV_W - _CBLK]], axis=-1)
    g = z8_ref[:, 2 * CONV_W - _CBLK:]
    ext[HIST:HIST + tt, :] = a * _sigmoid(bgate)

    base = HIST - (CONV_K - 1)
    n_sh = HIST + tt - sub
    for r in range(1, sub):
        shifted[r, 0:n_sh, :] = ext[r:r + n_sh, :]
    for ci in range(tt // CONV_RC):
        r0 = ci * CONV_RC
        acc = jnp.zeros((CONV_RC // sub, sub, CONV_W), F32)
        for k in range(CONV_K):
            r = (base + k) % sub
            a0 = r0 + (base + k) - r
            rows = ext[a0:a0 + CONV_RC, :] if r == 0 else shifted[r, a0:a0 + CONV_RC, :]
            acc = acc + wdw_ref[k][None] * rows.reshape(CONV_RC // sub, sub, CONV_W)
        ybuf[r0:r0 + CONV_RC, :] = acc.reshape(CONV_RC, CONV_W)
    y = ybuf[...] + bdw_ref[...]
    o_ref[...] = _conv_tail(y, g, ln_g_ref, ln_b_ref, wpw_ref, bpw_ref).astype(o_ref.dtype)

    @pl.when(t == nt - 1)
    def _():
        s_ref[0] = ext[tt + HIST - (CONV_K - 1):tt + HIST, :]

    ext[0:HIST, :] = ext[tt:tt + HIST, :]


def _conv_prefill(z, mix, w_dw, b_dw, ln_g, ln_b, wpw_all, b_pw, batch, seq, layer, s_all):
    tt = CONV_TT
    nt = seq // tt
    vec = lambda i, t: (0, 0)
    in_specs = [pl.BlockSpec((tt, _CBLK), lambda b, t: (b * nt + t, OFF_AC // _CBLK)),
                pl.BlockSpec((tt, _CBLK), lambda b, t: (b * nt + t, OFF_AC // _CBLK + 1)),
                pl.BlockSpec((CONV_K, 8, CONV_W), lambda b, t: (0, 0, 0)),
                pl.BlockSpec((1, CONV_W), vec),
                pl.BlockSpec((1, CONV_W), vec),
                pl.BlockSpec((1, CONV_W), vec),
                pl.BlockSpec((None, CONV_W, CONV_W), lambda b, t: (layer, 0, 0)),
                pl.BlockSpec((1, CONV_W), vec)]
    lent_specs, aliases, lent = _lend(len(in_specs), [mix, s_all], [0, 1])
    return pl.pallas_call(
        functools.partial(_conv_prefill_body, n_lent=len(lent)),
        grid=(batch, nt),
        in_specs=in_specs + lent_specs,
        out_specs=[pl.BlockSpec((tt, CONV_W), lambda b, t: (b * nt + t, (RET_W + HG_W) // CONV_W)),
                   pl.BlockSpec((None, 1, CONV_K - 1, CONV_W), lambda b, t: (layer, b, 0, 0))],
        out_shape=[jax.ShapeDtypeStruct(mix.shape, BF16),
                   jax.ShapeDtypeStruct((DEPTH, batch, CONV_K - 1, CONV_W), F32)],
        scratch_shapes=[pltpu.VMEM((HIST + tt, CONV_W), F32), pltpu.VMEM((8, HIST + tt, CONV_W), F32),
                        pltpu.VMEM((tt, CONV_W), F32)],
        input_output_aliases=aliases,
        compiler_params=_params(("parallel", "arbitrary")),
        name="conv_prefill",
    )(z, z, jnp.broadcast_to(w_dw[:, None, :], (CONV_K, 8, CONV_W)), b_dw.reshape(1, -1),
      ln_g.reshape(1, -1), ln_b.reshape(1, -1), wpw_all, b_pw.reshape(1, -1), *lent)


def _ret_decode_body(qT_ref, kT_ref, v_ref, g_ref, cos_ref, sin_ref, lg_ref, gn_ref, s_ref, *rest,
                     n_lent):
    o_ref, so_ref, orow = rest[n_lent:]
    half = RET_DK // 2
    cosc = cos_ref[...]
    sinc = sin_ref[...]
    for h in range(RET_HEADS):
        zq = qT_ref[h]
        zk = kT_ref[h]
        q = zq * cosc + jnp.concatenate([zq[half:], zq[:half]], axis=0) * sinc
        k = (zk * cosc + jnp.concatenate([zk[half:], zk[:half]], axis=0) * sinc) * (RET_DK ** -0.5)
        gamma = jnp.exp(lg_ref[h:h + 1, :])
        for j in range(DEC_BB):
            vrow = v_ref[j:j + 1, h * RET_DV:(h + 1) * RET_DV]
            s1 = gamma * s_ref[j, h] + k[:, j:j + 1] * vrow
            so_ref[j, h] = s1
            orow[j:j + 1, h * RET_DV:(h + 1) * RET_DV] = jnp.sum(q[:, j:j + 1] * s1, axis=0, keepdims=True)
    for h in range(RET_HEADS):
        sl = slice(h * RET_DV, (h + 1) * RET_DV)
        o = orow[:, sl]
        mu = jnp.mean(o, axis=-1, keepdims=True)
        var = jnp.mean(jnp.square(o - mu), axis=-1, keepdims=True)
        on = (o - mu) * lax.rsqrt(var + EPS) * gn_ref[:, sl]
        o_ref[:, sl] = on * _silu(g_ref[:, sl])


def _cols(zs, off, heads, dk):
    bsz = zs.shape[0]
    x = zs[:, off:off + heads * dk].reshape(bsz // DEC_BB, DEC_BB, heads, dk)
    return x.transpose(0, 2, 3, 1)


def _ret_decode(zs, state_all, layer, cosc, sinc, lg_t, gn_g, so_all):
    bsz = zs.shape[0]
    bb = DEC_BB
    qT = _cols(zs, OFF_QR, RET_HEADS, RET_DK)
    kT = _cols(zs, OFF_KR, RET_HEADS, RET_DK)
    col_spec = pl.BlockSpec((None, RET_HEADS, RET_DK, bb), lambda i: (i, 0, 0, 0))
    st_spec = pl.BlockSpec((None, bb, RET_HEADS, RET_DK, RET_DV), lambda i: (layer, i, 0, 0, 0))
    in_specs = [col_spec, col_spec,
                pl.BlockSpec((bb, RET_W), lambda i: (i, OFF_VR // RET_W)),
                pl.BlockSpec((bb, RET_W), lambda i: (i, OFF_GR // RET_W)),
                pl.BlockSpec((RET_DK, bb), lambda i: (0, 0)),
                pl.BlockSpec((RET_DK, bb), lambda i: (0, 0)),
                pl.BlockSpec((RET_HEADS, RET_DV), lambda i: (0, 0)),
                pl.BlockSpec((1, RET_W), lambda i: (0, 0)),
                st_spec]
    lent_specs, aliases, lent = _lend(len(in_specs), [so_all], [1])
    return pl.pallas_call(
        functools.partial(_ret_decode_body, n_lent=len(lent)),
        grid=(bsz // bb,),
        in_specs=in_specs + lent_specs,
        out_specs=[pl.BlockSpec((bb, RET_W), lambda i: (i, 0)), st_spec],
        out_shape=[jax.ShapeDtypeStruct((bsz, RET_W), F32),
                   jax.ShapeDtypeStruct(state_all.shape, F32)],
        scratch_shapes=[pltpu.VMEM((bb, RET_W), F32)],
        input_output_aliases=aliases,
        compiler_params=_params(("parallel",)),
        name="ret_decode",
    )(qT, kT, zs, zs, cosc, sinc, lg_t, gn_g.reshape(1, RET_W), state_all, *lent)


def _hgrn_decode_body(q_ref, fT_ref, v_ref, g_ref, lbT_ref, ng_ref, s_ref, *rest, n_lent):
    o_ref, so_ref, orow = rest[n_lent:]
    for h in range(HG_HEADS):
        sl = slice(h * HG_DV, (h + 1) * HG_DV)
        qb = _silu(q_ref[:, sl]).astype(BF16)
        lb = lbT_ref[h]
        f = lb + (1.0 - lb) * _sigmoid(fT_ref[h])
        for j in range(DEC_BB):
            vrow = v_ref[j:j + 1, sl]
            s1 = f[:, j:j + 1] * (s_ref[j, h] - vrow) + vrow
            so_ref[j, h] = s1
            oj = jnp.dot(qb, s1.astype(BF16), preferred_element_type=F32)
            orow[j:j + 1, sl] = oj[j:j + 1, :]
    for h in range(HG_HEADS):
        sl = slice(h * HG_DV, (h + 1) * HG_DV)
        o = orow[:, sl]
        on = o * lax.rsqrt(jnp.mean(o * o, axis=-1, keepdims=True) + EPS) * ng_ref[:, sl]
        o_ref[:, sl] = on * _silu(g_ref[:, sl])


def _hgrn_decode(zs, state_all, layer, lb, ng, so_all):
    bsz = zs.shape[0]
    bb = DEC_BB
    fT = _cols(zs, OFF_FH, HG_HEADS, HG_DK)
    col_spec = pl.BlockSpec((None, HG_HEADS, HG_DK, bb), lambda i: (i, 0, 0, 0))
    st_spec = pl.BlockSpec((None, bb, HG_HEADS, HG_DK, HG_DV), lambda i: (layer, i, 0, 0, 0))
    in_specs = [pl.BlockSpec((bb, HG_W), lambda i: (i, OFF_QH // HG_W)),
                col_spec,
                pl.BlockSpec((bb, HG_W), lambda i: (i, OFF_IH // HG_W)),
                pl.BlockSpec((bb, HG_W), lambda i: (i, OFF_GH // HG_W)),
                pl.BlockSpec((HG_HEADS, HG_DK, 1), lambda i: (0, 0, 0)),
                pl.BlockSpec((1, HG_W), lambda i: (0, 0)),
                st_spec]
    lent_specs, aliases, lent = _lend(len(in_specs), [so_all], [1])
    return pl.pallas_call(
        functools.partial(_hgrn_decode_body, n_lent=len(lent)),
        grid=(bsz // bb,),
        in_specs=in_specs + lent_specs,
        out_specs=[pl.BlockSpec((bb, HG_W), lambda i: (i, 0)), st_spec],
        out_shape=[jax.ShapeDtypeStruct((bsz, HG_W), F32),
                   jax.ShapeDtypeStruct(state_all.shape, F32)],
        scratch_shapes=[pltpu.VMEM((bb, HG_W), F32)],
        input_output_aliases=aliases,
        compiler_params=_params(("parallel",)),
        name="hgrn_decode",
    )(zs, fT, zs, zs, lb.reshape(HG_HEADS, HG_DK, 1), ng.reshape(1, HG_W), state_all, *lent)


def _conv_decode_body(z7_ref, z8_ref, wdw_ref, bdw_ref, ln_g_ref, ln_b_ref, wpw_ref, bpw_ref, s_ref,
                      *rest, n_lent):
    o_ref, so_ref = rest[n_lent:]
    kh = CONV_K - 1
    a = z7_ref[:, 0:CONV_W]
    bgate = jnp.concatenate([z7_ref[:, CONV_W:_CBLK], z8_ref[:, 0:2 * CONV_W - _CBLK]], axis=-1)
    g = z8_ref[:, 2 * CONV_W - _CBLK:]
    u = a * _sigmoid(bgate)
    buf = s_ref[...]
    y = jnp.sum(buf * wdw_ref[0:kh, :][None], axis=1) + u * wdw_ref[kh:kh + 1, :] + bdw_ref[...]
    o_ref[...] = _conv_tail(y, g, ln_g_ref, ln_b_ref, wpw_ref, bpw_ref)
    so_ref[:, 0:kh - 1, :] = s_ref[:, 1:kh, :]
    for j in range(DEC_BB):
        so_ref[j, kh - 1:kh, :] = u[j:j + 1, :]


def _conv_decode(zs, state_all, layer, w_dw, b_dw, ln_g, ln_b, wpw_all, b_pw, so_all):
    bsz = zs.shape[0]
    bb = DEC_BB
    vec = lambda i: (0, 0)
    st_spec = pl.BlockSpec((None, bb, CONV_K - 1, CONV_W), lambda i: (layer, i, 0, 0))
    in_specs = [pl.BlockSpec((bb, _CBLK), lambda i: (i, OFF_AC // _CBLK)),
                pl.BlockSpec((bb, _CBLK), lambda i: (i, OFF_AC // _CBLK + 1)),
                pl.BlockSpec((CONV_K, CONV_W), vec),
                pl.BlockSpec((1, CONV_W), vec),
                pl.BlockSpec((1, CONV_W), vec),
                pl.BlockSpec((1, CONV_W), vec),
                pl.BlockSpec((None, CONV_W, CONV_W), lambda i: (layer, 0, 0)),
                pl.BlockSpec((1, CONV_W), vec),
                st_spec]
    lent_specs, aliases, lent = _lend(len(in_specs), [so_all], [1])
    return pl.pallas_call(
        functools.partial(_conv_decode_body, n_lent=len(lent)),
        grid=(bsz // bb,),
        in_specs=in_specs + lent_specs,
        out_specs=[pl.BlockSpec((bb, CONV_W), lambda i: (i, 0)), st_spec],
        out_shape=[jax.ShapeDtypeStruct((bsz, CONV_W), F32),
                   jax.ShapeDtypeStruct(state_all.shape, F32)],
        input_output_aliases=aliases,
        compiler_params=_params(("parallel",)),
        name="conv_decode",
    )(zs, zs, w_dw, b_dw.reshape(1, -1), ln_g.reshape(1, -1), ln_b.reshape(1, -1), wpw_all,
      b_pw.reshape(1, -1), state_all, *lent)


_N_DEC_IN = 16


def _decode_one_sequence(ins, outs, scr):
    (qT_ref, kT_ref, rv_ref, rg_ref, cos_ref, sin_ref, lg_ref, gn_ref, rs_ref,
     hq_ref, fT_ref, hv_ref, hg_ref, lbT_ref, ng_ref, hs_ref) = ins
    ro_ref, rso_ref, ho_ref, hso_ref = outs
    rrow, hrow = scr
    half = RET_DK // 2
    cosc = cos_ref[...]
    sinc = sin_ref[...]
    for h in range(RET_HEADS):
        sl = slice(h * RET_DV, (h + 1) * RET_DV)
        zq = qT_ref[h]
        zk = kT_ref[h]
        q = zq * cosc + jnp.concatenate([zq[half:], zq[:half]], axis=0) * sinc
        k = (zk * cosc + jnp.concatenate([zk[half:], zk[:half]], axis=0) * sinc) * (RET_DK ** -0.5)
        s1 = jnp.exp(lg_ref[h:h + 1, :]) * rs_ref[h] + k * rv_ref[:, sl]
        rso_ref[h] = s1
        rrow[0:1, sl] = jnp.sum(q * s1, axis=0, keepdims=True)
    for h in range(RET_HEADS):
        sl = slice(h * RET_DV, (h + 1) * RET_DV)
        o = rrow[0:1, sl]
        mu = jnp.mean(o, axis=-1, keepdims=True)
        var = jnp.mean(jnp.square(o - mu), axis=-1, keepdims=True)
        on = (o - mu) * lax.rsqrt(var + EPS) * gn_ref[:, sl]
        ro_ref[:, sl] = on * _silu(rg_ref[:, sl])
    for h in range(HG_HEADS):
        sl = slice(h * HG_DV, (h + 1) * HG_DV)
        qb = jnp.broadcast_to(_silu(hq_ref[:, sl]), (8, HG_DK)).astype(BF16)
        lb = lbT_ref[h]
        f = lb + (1.0 - lb) * _sigmoid(fT_ref[h])
        vrow = hv_ref[:, sl]
        s1 = f * (hs_ref[h] - vrow) + vrow
        hso_ref[h] = s1
        hrow[0:1, sl] = jnp.dot(qb, s1.astype(BF16), preferred_element_type=F32)[0:1, :]
    for h in range(HG_HEADS):
        sl = slice(h * HG_DV, (h + 1) * HG_DV)
        o = hrow[0:1, sl]
        on = o * lax.rsqrt(jnp.mean(o * o, axis=-1, keepdims=True) + EPS) * ng_ref[:, sl]
        ho_ref[:, sl] = on * _silu(hg_ref[:, sl])


def _inproj_decode_body(x_ref, w_ref, *rest, n_lent):
    dec_in = rest[:_N_DEC_IN]
    z_ref, ro_ref, rso_ref, ho_ref, hso_ref, acc_ref, rrow, hrow = rest[_N_DEC_IN + n_lent:]
    k = pl.program_id(2)

    @pl.when(k == 0)
    def _():
        acc_ref[...] = jnp.dot(x_ref[...], w_ref[...], preferred_element_type=F32)
        _decode_one_sequence(dec_in, (ro_ref, rso_ref, ho_ref, hso_ref), (rrow, hrow))

    @pl.when(k == 1)
    def _():
        z_ref[...] = acc_ref[...] + jnp.dot(x_ref[...], w_ref[...], preferred_element_type=F32)
        _decode_one_sequence(dec_in, (ro_ref, rso_ref, ho_ref, hso_ref), (rrow, hrow))


def _inproj_with_decode(x, w_b, zs, state_ret, state_hgrn, layer, cos1, sin1, lg_t, gn_g, lb, ng,
                        rso_all, hso_all, *, tm, tn, tk):
    m, kdim = x.shape
    n = w_b.shape[-1]
    bsz = zs.shape[0]
    nj, nk = n // tn, kdim // tk
    assert nk == 2 and (m // tm) * nj * nk >= bsz
    seq = lambda i, j, k: jnp.minimum((i * nj + j) * nk + k, bsz - 1)
    z3 = zs.reshape(bsz, 1, IN_W)

    def cols(off, heads, dk):
        return zs[:, off:off + heads * dk].reshape(bsz, heads, dk, 1)

    def row_spec(width, off):
        return pl.BlockSpec((None, 1, width), lambda i, j, k: (seq(i, j, k), 0, off // width))

    def col_spec(heads, dk):
        return pl.BlockSpec((None, heads, dk, 1), lambda i, j, k: (seq(i, j, k), 0, 0, 0))

    def state_spec(heads, dk, dv):
        return pl.BlockSpec((None, None, heads, dk, dv), lambda i, j, k: (layer, seq(i, j, k), 0, 0, 0))

    const2 = lambda i, j, k: (0, 0)
    in_specs = [pl.BlockSpec((tm, tk), lambda i, j, k: (i, k)),
                pl.BlockSpec((tk, tn), lambda i, j, k: (k, j)),
                col_spec(RET_HEADS, RET_DK), col_spec(RET_HEADS, RET_DK),
                row_spec(RET_W, OFF_VR), row_spec(RET_W, OFF_GR),
                pl.BlockSpec((RET_DK, 1), const2), pl.BlockSpec((RET_DK, 1), const2),
                pl.BlockSpec((RET_HEADS, RET_DV), const2), pl.BlockSpec((1, RET_W), const2),
                state_spec(RET_HEADS, RET_DK, RET_DV),
                row_spec(HG_W, OFF_QH), col_spec(HG_HEADS, HG_DK),
                row_spec(HG_W, OFF_IH), row_spec(HG_W, OFF_GH),
                pl.BlockSpec((HG_HEADS, HG_DK, 1), lambda i, j, k: (0, 0, 0)),
                pl.BlockSpec((1, HG_W), const2),
                state_spec(HG_HEADS, HG_DK, HG_DV)]
    args = [x, w_b,
            cols(OFF_QR, RET_HEADS, RET_DK), cols(OFF_KR, RET_HEADS, RET_DK), z3, z3,
            cos1, sin1, lg_t, gn_g.reshape(1, RET_W), state_ret,
            z3, cols(OFF_FH, HG_HEADS, HG_DK), z3, z3,
            lb.reshape(HG_HEADS, HG_DK, 1), ng.reshape(1, HG_W), state_hgrn]
    assert len(in_specs) == 2 + _N_DEC_IN
    lent_specs, aliases, lent = _lend(len(in_specs), [rso_all, hso_all], [2, 4])
    orow_spec = lambda width: pl.BlockSpec((None, 1, width), lambda i, j, k: (seq(i, j, k), 0, 0))
    outs = pl.pallas_call(
        functools.partial(_inproj_decode_body, n_lent=len(lent)),
        grid=(m // tm, nj, nk),
        in_specs=in_specs + lent_specs,
        out_specs=[pl.BlockSpec((tm, tn), lambda i, j, k: (i, j)),
                   orow_spec(RET_W), state_spec(RET_HEADS, RET_DK, RET_DV),
                   orow_spec(HG_W), state_spec(HG_HEADS, HG_DK, HG_DV)],
        out_shape=[jax.ShapeDtypeStruct((m, n), F32),
                   jax.ShapeDtypeStruct((bsz, 1, RET_W), F32),
                   jax.ShapeDtypeStruct(state_ret.shape, F32),
                   jax.ShapeDtypeStruct((bsz, 1, HG_W), F32),
                   jax.ShapeDtypeStruct(state_hgrn.shape, F32)],
        scratch_shapes=[pltpu.VMEM((tm, tn), F32), pltpu.VMEM((8, RET_W), F32), pltpu.VMEM((8, HG_W), F32)],
        input_output_aliases=aliases,
        compiler_params=_params(("arbitrary", "arbitrary", "arbitrary")),
        name="inproj_decode",
    )(*args, *lent)
    z, ro, rso, ho, hso = outs
    return z, ro.reshape(bsz, RET_W), rso, ho.reshape(bsz, HG_W), hso


def _rope_tables(pos):
    half = RET_DK // 2
    inv = ROPE_BASE ** (-jnp.arange(half, dtype=F32) / half)
    ang = pos[:, None] * inv[None, :]
    cos = jnp.cos(ang)
    sin = jnp.sin(ang)
    return jnp.concatenate([cos, cos], axis=-1), jnp.concatenate([-sin, sin], axis=-1)


PROMPT_TILES_IN = (1024, 1536, 2048)
PROMPT_TILES_SQ = (1024, 512, D_MODEL)
SAMPLE_TN_IN = 768
SAMPLE_TN_SQ = 512


def kernel(x_prompt, x_sample, state_ret, state_hgrn, state_conv, p_prompt, p_sample, norm_g, w_in,
           ret_gn_g, hg_lower_bounds, hg_norm_g, w_dw, b_dw, conv_ln_g, conv_ln_b, w_pw, b_pw, w_out,
           w_ple, w_pg, final_norm_g):
    bp, tp, d = x_prompt.shape
    bs, ts, _ = x_sample.shape
    mp, ms = bp * tp, bs * ts
    assert ts == 1 and tp % CONV_TT == 0 and bs % DEC_BB == 0

    w_ple_b = w_ple.astype(BF16)
    w_pw_b = w_pw.astype(BF16)
    pp_b = p_prompt.reshape(DEPTH, mp, PLE_DIM).astype(BF16)
    ps_b = p_sample.reshape(DEPTH, ms, PLE_DIM).astype(BF16)

    lbs = jax.nn.softmax(hg_lower_bounds.astype(F32), axis=0)
    lbs = jnp.cumsum(lbs, axis=0) - lbs[0:1]
    log_gamma = np.log(1.0 - 2.0 ** (-5.0 - np.arange(RET_HEADS, dtype=np.float32))).astype(np.float32)
    lg_t = jnp.asarray(np.broadcast_to(log_gamma[:, None], (RET_HEADS, RET_DV)))

    cos_p, sin_p = _rope_tables(jnp.arange(tp, dtype=F32))
    cos_s, sin_s = _rope_tables(PAST_LEN + jnp.arange(ts, dtype=F32))
    cosc = jnp.broadcast_to(cos_s.reshape(RET_DK, 1), (RET_DK, DEC_BB))
    sinc = jnp.broadcast_to(sin_s.reshape(RET_DK, 1), (RET_DK, DEC_BB))

    cos1, sin1 = cos_s.reshape(RET_DK, 1), sin_s.reshape(RET_DK, 1)

    def layer(hp, hs, l, st_p, st_s):
        pr, ph, pc = st_p
        sr, sh, sc = st_s
        xs = _rmsnorm(hs, norm_g[l], BF16, ms)
        zs, w_in_b = _mm(xs, w_in, l, mode="plain", tm=ms, tn=SAMPLE_TN_IN, tk=d)

        tm, tn, tk = PROMPT_TILES_IN
        xp = _rmsnorm(hp, norm_g[l], BF16, 256)
        zp, mix_r, sr, mix_h, sh = _inproj_with_decode(
            xp, w_in_b, zs, state_ret, state_hgrn, l, cos1, sin1, lg_t, ret_gn_g[l], lbs[l],
            hg_norm_g[l], sr, sh, tm=tm, tn=tn, tk=tk)

        mix_c, sc = _conv_decode(zs, state_conv, l, w_dw[l], b_dw[l], conv_ln_g[l], conv_ln_b[l],
                                 w_pw_b, b_pw[l], sc)
        mix_s = jnp.concatenate([mix_r, mix_h, mix_c], axis=-1).astype(BF16)
        h1, h1b, w_out_b = _mm(mix_s, w_out, l, mode="resid", tm=ms, tn=SAMPLE_TN_SQ, tk=d, resid=hs)
        hs, w_pg_b = _mm(h1b, w_pg, l, mode="ple", tm=ms, tn=SAMPLE_TN_SQ, tk=d, resid=h1, p=ps_b,
                         wp_all=w_ple_b)

        mix, pr = _ret_prefill(zp, cos_p, sin_p, lg_t, ret_gn_g[l], bp, tp, l, pr)
        mix, ph = _hgrn_prefill(zp, mix, lbs[l], hg_norm_g[l], bp, tp, l, ph)
        mix, pc = _conv_prefill(zp, mix, w_dw[l], b_dw[l], conv_ln_g[l], conv_ln_b[l], w_pw_b,
                                b_pw[l], bp, tp, l, pc)
        tm, tn, tk = PROMPT_TILES_SQ
        h1, h1b = _mm(mix, w_out_b, l, mode="resid", tm=tm, tn=tn, tk=tk, resid=hp)
        (hp,) = _mm(h1b, w_pg_b, l, mode="ple", tm=tm, tn=tn, tk=tk, resid=h1, p=pp_b, wp_all=w_ple_b)
        return hp, hs, (pr, ph, pc), (sr, sh, sc)

    hp, hs = x_prompt.reshape(mp, d), x_sample.reshape(ms, d)
    st_p = st_s = (None, None, None)
    for l in range(DEPTH):
        hp, hs, st_p, st_s = layer(hp, hs, l, st_p, st_s)
    y_p = _rmsnorm(hp, final_norm_g, F32, 256).reshape(x_prompt.shape)
    y_s = _rmsnorm(hs, final_norm_g, F32, ms).reshape(x_sample.shape)
    return (y_p, y_s) + st_p + st_s
```

```python
import functools
import math

import numpy as np
import jax
import jax.numpy as jnp
from jax import lax
from jax.experimental import pallas as pl
from jax.experimental.pallas import tpu as pltpu

F32 = jnp.float32
BF16 = jnp.bfloat16

D_MODEL = 4096
DEPTH = 4
PAST_LEN = 16384
PLE_DIM = 256
RET_HEADS = 6
RET_DK = 128
RET_DV = 256
RET_W = RET_HEADS * RET_DV
ROPE_BASE = 10000.0
HG_HEADS = 12
HG_DK = 128
HG_DV = 128
HG_W = HG_HEADS * HG_DV
CONV_W = D_MODEL - RET_W - HG_W
CONV_K = 31
EPS = 1e-6
IN_W = 2 * RET_HEADS * RET_DK + 2 * RET_W + 2 * HG_HEADS * HG_DK + 2 * HG_W + 3 * CONV_W

OFF_QR = 0
OFF_KR = OFF_QR + RET_HEADS * RET_DK
OFF_VR = OFF_KR + RET_HEADS * RET_DK
OFF_GR = OFF_VR + RET_W
OFF_QH = OFF_GR + RET_W
OFF_FH = OFF_QH + HG_HEADS * HG_DK
OFF_IH = OFF_FH + HG_HEADS * HG_DK
OFF_GH = OFF_IH + HG_W
OFF_AC = OFF_GH + HG_W
OFF_BC = OFF_AC + CONV_W
OFF_GC = OFF_BC + CONV_W

V7X_VMEM_LIMIT_BYTES = 56 * 1024 * 1024
CHUNK = 128
HIST = 32
DEC_BB = 8

_NT = (((1,), (1,)), ((), ()))
_ANY = pl.BlockSpec(memory_space=pl.ANY)


def _params(sem):
    return pltpu.CompilerParams(dimension_semantics=sem, vmem_limit_bytes=V7X_VMEM_LIMIT_BYTES)


def _sigmoid(x):
    return 0.5 * jnp.tanh(0.5 * x) + 0.5


def _silu(x):
    return x * _sigmoid(x)


def _lend(n_in, carried, out_positions):
    specs, aliases = [], {}
    for buf, out_pos in zip(carried, out_positions):
        if buf is not None:
            aliases[n_in + len(specs)] = out_pos
            specs.append(_ANY)
    return specs, aliases, [b for b in carried if b is not None]


def _rmsnorm_body(x_ref, g_ref, o_ref):
    x = x_ref[...]
    ms = jnp.mean(x * x, axis=-1, keepdims=True)
    o_ref[...] = (x * lax.rsqrt(ms + EPS) * g_ref[...]).astype(o_ref.dtype)


def _rmsnorm(x, g, out_dtype, tm):
    m, d = x.shape
    return pl.pallas_call(
        _rmsnorm_body,
        grid=(m // tm,),
        in_specs=[pl.BlockSpec((tm, d), lambda i: (i, 0)),
                  pl.BlockSpec((1, d), lambda i: (0, 0))],
        out_specs=pl.BlockSpec((tm, d), lambda i: (i, 0)),
        out_shape=jax.ShapeDtypeStruct((m, d), out_dtype),
        compiler_params=_params(("parallel",)),
        name="rmsnorm",
    )(x, g.reshape(1, d))


def _mm_body(x_ref, w_ref, *rest, nk, mode, cast_w):
    if nk > 1:
        acc_ref = rest[-1]
        rest = rest[:-1]
    if cast_w:
        wb_ref = rest[-1]
        rest = rest[:-1]

    def partial_product():
        w = w_ref[...]
        if cast_w:
            w = w.astype(BF16)
            wb_ref[...] = w
        return jnp.dot(x_ref[...], w, preferred_element_type=F32)

    def epilogue(acc):
        if mode == "plain":
            (o_ref,) = rest
            o_ref[...] = acc
        elif mode == "resid":
            r_ref, o_ref, ob_ref = rest
            h1 = r_ref[...] + acc
            o_ref[...] = h1
            ob_ref[...] = h1.astype(BF16)
        else:
            r_ref, p_ref, wp_ref, o_ref = rest
            ple = jnp.dot(p_ref[...], wp_ref[...], preferred_element_type=F32)
            o_ref[...] = r_ref[...] + _sigmoid(acc) * ple

    if nk == 1:
        epilogue(partial_product())
        return
    k = pl.program_id(2)

    @pl.when(k == 0)
    def _():
        acc_ref[...] = partial_product()

    if nk > 2:
        @pl.when(jnp.logical_and(k > 0, k < nk - 1))
        def _():
            acc_ref[...] += partial_product()

    @pl.when(k == nk - 1)
    def _():
        epilogue(acc_ref[...] + partial_product())


def _mm(x, w, layer, *, mode, tm, tn, tk, resid=None, p=None, wp_all=None):
    m, kdim = x.shape
    n = w.shape[-1]
    nk = kdim // tk
    cast_w = w.ndim == 3
    assert not cast_w or (nk == 1 and m == tm)
    grid = (m // tm, n // tn, nk)
    w_spec = (pl.BlockSpec((None, tk, tn), lambda i, j, k: (layer, k, j)) if cast_w
              else pl.BlockSpec((tk, tn), lambda i, j, k: (k, j)))
    in_specs = [pl.BlockSpec((tm, tk), lambda i, j, k: (i, k)), w_spec]
    args = [x, w]
    o_spec = pl.BlockSpec((tm, tn), lambda i, j, k: (i, j))
    out_specs = [o_spec]
    out_shape = [jax.ShapeDtypeStruct((m, n), F32)]
    if mode == "resid":
        in_specs.append(o_spec)
        args.append(resid)
        out_specs.append(o_spec)
        out_shape.append(jax.ShapeDtypeStruct((m, n), BF16))
    elif mode == "ple":
        pdim = p.shape[-1]
        in_specs += [o_spec,
                     pl.BlockSpec((None, tm, pdim), lambda i, j, k: (layer, i, 0)),
                     pl.BlockSpec((None, pdim, tn), lambda i, j, k: (layer, 0, j))]
        args += [resid, p, wp_all]
    if cast_w:
        out_specs.append(pl.BlockSpec((tk, tn), lambda i, j, k: (k, j)))
        out_shape.append(jax.ShapeDtypeStruct((kdim, n), BF16))
    return pl.pallas_call(
        functools.partial(_mm_body, nk=nk, mode=mode, cast_w=cast_w),
        grid=grid,
        in_specs=in_specs,
        out_specs=out_specs,
        out_shape=out_shape,
        scratch_shapes=[pltpu.VMEM((tm, tn), F32)] if nk > 1 else [],
        compiler_params=_params(("parallel", "parallel", "arbitrary")),
        name="mm_" + mode + ("_cast" if cast_w else ""),
    )(*args)


def _ret_prefill_body(q_ref, k_ref, v_ref, g_ref, cos_ref, sin_ref, lg_ref, gn_ref, *rest, n_lent):
    o_ref, s_ref, st_scr = rest[n_lent:]
    c = pl.program_id(1)
    nc = pl.num_programs(1)
    L = CHUNK

    @pl.when(c == 0)
    def _():
        st_scr[...] = jnp.zeros_like(st_scr)

    cos = cos_ref[...]
    sin = sin_ref[...]
    row = lax.broadcasted_iota(jnp.int32, (L, L), 0)
    col = lax.broadcasted_iota(jnp.int32, (L, L), 1)
    diff = (row - col).astype(F32)
    causal = row >= col
    rowf = lax.broadcasted_iota(jnp.int32, (L, RET_DK), 0).astype(F32)
    for h in range(RET_HEADS):
        lg = lg_ref[h:h + 1, :]
        lg1 = lg[:, :RET_DK]
        decay_in = jnp.where(causal, jnp.exp(jnp.where(causal, diff, 0.0) * lg1), 0.0)
        q_decay = jnp.exp((rowf + 1.0) * lg1)
        k_decay = jnp.exp((L - 1.0 - rowf) * lg1)
        chunk_decay = jnp.exp(L * lg)

        zq = q_ref[:, h * RET_DK:(h + 1) * RET_DK]
        zk = k_ref[:, h * RET_DK:(h + 1) * RET_DK]
        v = v_ref[:, h * RET_DV:(h + 1) * RET_DV]
        zg = g_ref[:, h * RET_DV:(h + 1) * RET_DV]
        q = zq * cos + pltpu.roll(zq, RET_DK // 2, 1) * sin
        k = (zk * cos + pltpu.roll(zk, RET_DK // 2, 1) * sin) * (RET_DK ** -0.5)
        vb = v.astype(BF16)
        s0 = st_scr[h]

        scores = lax.dot_general(q.astype(BF16), k.astype(BF16), _NT,
                                 preferred_element_type=F32) * decay_in
        o = (jnp.dot(scores.astype(BF16), vb, preferred_element_type=F32)
             + jnp.dot((q * q_decay).astype(BF16), s0.astype(BF16), preferred_element_type=F32))
        kT = (k * k_decay).T.astype(BF16)
        st_scr[h] = s0 * chunk_decay + jnp.dot(kT, vb, preferred_element_type=F32)

        mu = jnp.mean(o, axis=-1, keepdims=True)
        var = jnp.mean(jnp.square(o - mu), axis=-1, keepdims=True)
        on = (o - mu) * lax.rsqrt(var + EPS) * gn_ref[:, h * RET_DV:(h + 1) * RET_DV]
        o_ref[:, h * RET_DV:(h + 1) * RET_DV] = (on * _silu(zg)).astype(o_ref.dtype)

    @pl.when(c == nc - 1)
    def _():
        s_ref[0] = st_scr[...]


def _ret_prefill(z, cos_t, sin_t, lg_t, gn_g, batch, seq, layer, s_all):
    m = z.shape[0]
    nc = seq // CHUNK
    L = CHUNK
    rowmap = lambda blk: (lambda b, c: (b * nc + c, blk))
    in_specs = [pl.BlockSpec((L, RET_HEADS * RET_DK), rowmap(OFF_QR // (RET_HEADS * RET_DK))),
                pl.BlockSpec((L, RET_HEADS * RET_DK), rowmap(OFF_KR // (RET_HEADS * RET_DK))),
                pl.BlockSpec((L, RET_W), rowmap(OFF_VR // RET_W)),
                pl.BlockSpec((L, RET_W), rowmap(OFF_GR // RET_W)),
                pl.BlockSpec((L, RET_DK), lambda b, c: (c, 0)),
                pl.BlockSpec((L, RET_DK), lambda b, c: (c, 0)),
                pl.BlockSpec((RET_HEADS, RET_DV), lambda b, c: (0, 0)),
                pl.BlockSpec((1, RET_W), lambda b, c: (0, 0))]
    lent_specs, aliases, lent = _lend(len(in_specs), [s_all], [1])
    return pl.pallas_call(
        functools.partial(_ret_prefill_body, n_lent=len(lent)),
        grid=(batch, nc),
        in_specs=in_specs + lent_specs,
        out_specs=[pl.BlockSpec((L, RET_W), lambda b, c: (b * nc + c, 0)),
                   pl.BlockSpec((None, 1, RET_HEADS, RET_DK, RET_DV), lambda b, c: (layer, b, 0, 0, 0))],
        out_shape=[jax.ShapeDtypeStruct((m, D_MODEL), BF16),
                   jax.ShapeDtypeStruct((DEPTH, batch, RET_HEADS, RET_DK, RET_DV), F32)],
        scratch_shapes=[pltpu.VMEM((RET_HEADS, RET_DK, RET_DV), F32)],
        input_output_aliases=aliases,
        compiler_params=_params(("parallel", "arbitrary")),
        name="ret_prefill",
    )(z, z, z, z, cos_t, sin_t, lg_t, gn_g.reshape(1, RET_W), *lent)


HG_HB = 12
_NLEV = int(math.log2(CHUNK))


def _level_table():
    i = np.arange(CHUNK)[:, None]
    j = np.arange(CHUNK)[None, :]
    x = np.bitwise_xor(i, j)
    lvl = np.floor(np.log2(np.maximum(x, 1))).astype(np.int32)
    lvl = np.where(i == j, _NLEV, lvl)
    lvl = np.where(i < j, -1, lvl)
    return lvl.astype(np.int32)


def _split3(x):
    x1 = x.astype(BF16)
    r1 = x - x1.astype(F32)
    x2 = r1.astype(BF16)
    x3 = (r1 - x2.astype(F32)).astype(BF16)
    return x1, x2, x3


def _ref_rows(b, m):
    L, w = b.shape
    g = 2 * m
    b3 = b.reshape(L // g, g, w)
    r = jnp.broadcast_to(b3[:, m - 1:m, :], b3.shape)
    return r.reshape(L, w)


def _hgrn_prefill_body(q_ref, f_ref, i_ref, g_ref, lb_ref, ng_ref, tri_ref, lvl_ref, *rest, n_lent):
    o_ref, s_ref, st_scr = rest[n_lent:]
    c = pl.program_id(2)
    nc = pl.num_programs(2)
    L = CHUNK

    @pl.when(c == 0)
    def _():
        st_scr[...] = jnp.zeros_like(st_scr)

    lvl = lvl_ref[...]
    rowi = lax.broadcasted_iota(jnp.int32, (L, HG_DK), 0)
    odd = (rowi & 1) == 1
    r4 = rowi & 3
    r4_is0, r4_is1, r4_is2 = r4 == 0, r4 == 1, r4 == 2

    lb_all = lb_ref[...]
    f_all = lb_all + (1.0 - lb_all) * _sigmoid(f_ref[...])
    l1, l2, l3 = _split3(jnp.log(f_all))
    b_parts = jnp.dot(tri_ref[...], jnp.concatenate([l1, l2, l3], axis=1), preferred_element_type=F32)
    wblk = HG_HB * HG_DK
    b_all = b_parts[:, 0:wblk] + b_parts[:, wblk:2 * wblk] + b_parts[:, 2 * wblk:]
    for hh in range(HG_HB):
        sl = slice(hh * HG_DK, (hh + 1) * HG_DK)
        v = i_ref[:, sl]
        zg = g_ref[:, sl]
        q = _silu(q_ref[:, sl])
        f = f_all[:, sl]
        k = 1.0 - f
        b = b_all[:, sl]
        qb = q.astype(BF16)
        kb = k.astype(BF16)
        vb = v.astype(BF16)

        a = jnp.where(lvl == _NLEV, lax.dot_general(qb, kb, _NT, preferred_element_type=F32), 0.0)
        f_prev = pltpu.roll(f, 1, 0)
        f_next = pltpu.roll(f, L - 1, 0)
        for t in range(_NLEV):
            m = 1 << t
            if m == 1:
                e = jnp.where(odd, f, 1.0)
            elif m == 2:
                e = jnp.where(r4_is0, f_next, jnp.where(r4_is1, 1.0, jnp.where(r4_is2, f, f * f_prev)))
            else:
                e = jnp.exp(-jnp.abs(b - _ref_rows(b, m)))
            eb = e.astype(BF16)
            p = lax.dot_general(qb * eb, kb * eb, _NT, preferred_element_type=F32)
            a = jnp.where(lvl == t, p, a)

        st = st_scr[hh]
        o = (jnp.dot(a.astype(BF16), vb, preferred_element_type=F32)
             + lax.dot_general((q * jnp.exp(b)).astype(BF16), st.astype(BF16), _NT,
                               preferred_element_type=F32))
        b_last = b[L - 1:L, :]
        k_s = (k * jnp.exp(b_last - b)).astype(BF16)
        st_scr[hh] = st * jnp.exp(b_last) + jnp.dot(v.T.astype(BF16), k_s, preferred_element_type=F32)

        on = o * lax.rsqrt(jnp.mean(o * o, axis=-1, keepdims=True) + EPS) * ng_ref[:, sl]
        o_ref[:, sl] = (on * _silu(zg)).astype(o_ref.dtype)

    @pl.when(c == nc - 1)
    def _():
        for hh in range(HG_HB):
            s_ref[0, hh] = st_scr[hh].T


def _hgrn_prefill(z, mix, lb, ng, batch, seq, layer, s_all):
    nc = seq // CHUNK
    L = CHUNK
    ng_groups = HG_HEADS // HG_HB
    wblk = HG_HB * HG_DK
    rowmap = lambda off: (lambda b, g, c: (b * nc + c, off // wblk + g))
    tri = jnp.asarray(np.tril(np.ones((L, L), np.float32)), dtype=BF16)
    lvl = jnp.asarray(_level_table())
    in_specs = [pl.BlockSpec((L, wblk), rowmap(OFF_QH)),
                pl.BlockSpec((L, wblk), rowmap(OFF_FH)),
                pl.BlockSpec((L, wblk), rowmap(OFF_IH)),
                pl.BlockSpec((L, wblk), rowmap(OFF_GH)),
                pl.BlockSpec((1, wblk), lambda b, g, c: (0, g)),
                pl.BlockSpec((1, wblk), lambda b, g, c: (0, g)),
                pl.BlockSpec((L, L), lambda b, g, c: (0, 0)),
                pl.BlockSpec((L, L), lambda b, g, c: (0, 0))]
    lent_specs, aliases, lent = _lend(len(in_specs), [mix, s_all], [0, 1])
    return pl.pallas_call(
        functools.partial(_hgrn_prefill_body, n_lent=len(lent)),
        grid=(batch, ng_groups, nc),
        in_specs=in_specs + lent_specs,
        out_specs=[pl.BlockSpec((L, wblk), rowmap(RET_W)),
                   pl.BlockSpec((None, 1, HG_HB, HG_DK, HG_DV), lambda b, g, c: (layer, b, g, 0, 0))],
        out_shape=[jax.ShapeDtypeStruct(mix.shape, BF16),
                   jax.ShapeDtypeStruct((DEPTH, batch, HG_HEADS, HG_DK, HG_DV), F32)],
        scratch_shapes=[pltpu.VMEM((HG_HB, HG_DV, HG_DK), F32)],
        input_output_aliases=aliases,
        compiler_params=_params(("parallel", "parallel", "arbitrary")),
        name="hgrn_prefill",
    )(z, z, z, z, lb.reshape(1, HG_W), ng.reshape(1, HG_W), tri, lvl, *lent)


CONV_TT = 256
CONV_RC = 32
_CBLK = 1536


def _conv_tail(y, g, ln_g_ref, ln_b_ref, wpw_ref, bpw_ref):
    mu = jnp.mean(y, axis=-1, keepdims=True)
    var = jnp.mean(jnp.square(y - mu), axis=-1, keepdims=True)
    yn = (y - mu) * lax.rsqrt(var + EPS) * ln_g_ref[...] + ln_b_ref[...]
    pw = jnp.dot(_silu(yn).astype(BF16), wpw_ref[...], preferred_element_type=F32) + bpw_ref[...]
    return pw * _silu(g)


def _conv_prefill_body(z7_ref, z8_ref, wdw_ref, bdw_ref, ln_g_ref, ln_b_ref, wpw_ref, bpw_ref,
                       *rest, n_lent):
    o_ref, s_ref, ext, shifted, ybuf = rest[n_lent:]
    t = pl.program_id(1)
    nt = pl.num_programs(1)
    tt = CONV_TT
    sub = 8

    @pl.when(t == 0)
    def _():
        ext[0:HIST, :] = jnp.zeros((HIST, CONV_W), F32)

    a = z7_ref[:, 0:CONV_W]
    bgate = jnp.concatenate([z7_ref[:, CONV_W:_CBLK], z8_ref[:, 0:2 * CONV_W - _CBLK]], axis=-1)
    g = z8_ref[:, 2 * CONV_W - _CBLK:]
    ext[HIST:HIST + tt, :] = a * _sigmoid(bgate)

    base = HIST - (CONV_K - 1)
    n_sh = HIST + tt - sub
    for r in range(1, sub):
        shifted[r, 0:n_sh, :] = ext[r:r + n_sh, :]
    for ci in range(tt // CONV_RC):
        r0 = ci * CONV_RC
        acc = jnp.zeros((CONV_RC // sub, sub, CONV_W), F32)
        for k in range(CONV_K):
            r = (base + k) % sub
            a0 = r0 + (base + k) - r
            rows = ext[a0:a0 + CONV_RC, :] if r == 0 else shifted[r, a0:a0 + CONV_RC, :]
            acc = acc + wdw_ref[k][None] * rows.reshape(CONV_RC // sub, sub, CONV_W)
        ybuf[r0:r0 + CONV_RC, :] = acc.reshape(CONV_RC, CONV_W)
    y = ybuf[...] + bdw_ref[...]
    o_ref[...] = _conv_tail(y, g, ln_g_ref, ln_b_ref, wpw_ref, bpw_ref).astype(o_ref.dtype)

    @pl.when(t == nt - 1)
    def _():
        s_ref[0] = ext[tt + HIST - (CONV_K - 1):tt + HIST, :]

    ext[0:HIST, :] = ext[tt:tt + HIST, :]


def _conv_prefill(z, mix, w_dw, b_dw, ln_g, ln_b, wpw_all, b_pw, batch, seq, layer, s_all):
    tt = CONV_TT
    nt = seq // tt
    vec = lambda i, t: (0, 0)
    in_specs = [pl.BlockSpec((tt, _CBLK), lambda b, t: (b * nt + t, OFF_AC // _CBLK)),
                pl.BlockSpec((tt, _CBLK), lambda b, t: (b * nt + t, OFF_AC // _CBLK + 1)),
                pl.BlockSpec((CONV_K, 8, CONV_W), lambda b, t: (0, 0, 0)),
                pl.BlockSpec((1, CONV_W), vec),
                pl.BlockSpec((1, CONV_W), vec),
                pl.BlockSpec((1, CONV_W), vec),
                pl.BlockSpec((None, CONV_W, CONV_W), lambda b, t: (layer, 0, 0)),
                pl.BlockSpec((1, CONV_W), vec)]
    lent_specs, aliases, lent = _lend(len(in_specs), [mix, s_all], [0, 1])
    return pl.pallas_call(
        functools.partial(_conv_prefill_body, n_lent=len(lent)),
        grid=(batch, nt),
        in_specs=in_specs + lent_specs,
        out_specs=[pl.BlockSpec((tt, CONV_W), lambda b, t: (b * nt + t, (RET_W + HG_W) // CONV_W)),
                   pl.BlockSpec((None, 1, CONV_K - 1, CONV_W), lambda b, t: (layer, b, 0, 0))],
        out_shape=[jax.ShapeDtypeStruct(mix.shape, BF16),
                   jax.ShapeDtypeStruct((DEPTH, batch, CONV_K - 1, CONV_W), F32)],
        scratch_shapes=[pltpu.VMEM((HIST + tt, CONV_W), F32), pltpu.VMEM((8, HIST + tt, CONV_W), F32),
                        pltpu.VMEM((tt, CONV_W), F32)],
        input_output_aliases=aliases,
        compiler_params=_params(("parallel", "arbitrary")),
        name="conv_prefill",
    )(z, z, jnp.broadcast_to(w_dw[:, None, :], (CONV_K, 8, CONV_W)), b_dw.reshape(1, -1),
      ln_g.reshape(1, -1), ln_b.reshape(1, -1), wpw_all, b_pw.reshape(1, -1), *lent)


def _ret_decode_body(qT_ref, kT_ref, v_ref, g_ref, cos_ref, sin_ref, lg_ref, gn_ref, s_ref, *rest,
                     n_lent):
    o_ref, so_ref, orow = rest[n_lent:]
    half = RET_DK // 2
    cosc = cos_ref[...]
    sinc = sin_ref[...]
    for h in range(RET_HEADS):
        zq = qT_ref[h]
        zk = kT_ref[h]
        q = zq * cosc + jnp.concatenate([zq[half:], zq[:half]], axis=0) * sinc
        k = (zk * cosc + jnp.concatenate([zk[half:], zk[:half]], axis=0) * sinc) * (RET_DK ** -0.5)
        gamma = jnp.exp(lg_ref[h:h + 1, :])
        for j in range(DEC_BB):
            vrow = v_ref[j:j + 1, h * RET_DV:(h + 1) * RET_DV]
            s1 = gamma * s_ref[j, h] + k[:, j:j + 1] * vrow
            so_ref[j, h] = s1
            orow[j:j + 1, h * RET_DV:(h + 1) * RET_DV] = jnp.sum(q[:, j:j + 1] * s1, axis=0, keepdims=True)
    for h in range(RET_HEADS):
        sl = slice(h * RET_DV, (h + 1) * RET_DV)
        o = orow[:, sl]
        mu = jnp.mean(o, axis=-1, keepdims=True)
        var = jnp.mean(jnp.square(o - mu), axis=-1, keepdims=True)
        on = (o - mu) * lax.rsqrt(var + EPS) * gn_ref[:, sl]
        o_ref[:, sl] = on * _silu(g_ref[:, sl])


def _cols(zs, off, heads, dk):
    bsz = zs.shape[0]
    x = zs[:, off:off + heads * dk].reshape(bsz // DEC_BB, DEC_BB, heads, dk)
    return x.transpose(0, 2, 3, 1)


def _ret_decode(zs, state_all, layer, cosc, sinc, lg_t, gn_g, so_all):
    bsz = zs.shape[0]
    bb = DEC_BB
    qT = _cols(zs, OFF_QR, RET_HEADS, RET_DK)
    kT = _cols(zs, OFF_KR, RET_HEADS, RET_DK)
    col_spec = pl.BlockSpec((None, RET_HEADS, RET_DK, bb), lambda i: (i, 0, 0, 0))
    st_spec = pl.BlockSpec((None, bb, RET_HEADS, RET_DK, RET_DV), lambda i: (layer, i, 0, 0, 0))
    in_specs = [col_spec, col_spec,
                pl.BlockSpec((bb, RET_W), lambda i: (i, OFF_VR // RET_W)),
                pl.BlockSpec((bb, RET_W), lambda i: (i, OFF_GR // RET_W)),
                pl.BlockSpec((RET_DK, bb), lambda i: (0, 0)),
                pl.BlockSpec((RET_DK, bb), lambda i: (0, 0)),
                pl.BlockSpec((RET_HEADS, RET_DV), lambda i: (0, 0)),
                pl.BlockSpec((1, RET_W), lambda i: (0, 0)),
                st_spec]
    lent_specs, aliases, lent = _lend(len(in_specs), [so_all], [1])
    return pl.pallas_call(
        functools.partial(_ret_decode_body, n_lent=len(lent)),
        grid=(bsz // bb,),
        in_specs=in_specs + lent_specs,
        out_specs=[pl.BlockSpec((bb, RET_W), lambda i: (i, 0)), st_spec],
        out_shape=[jax.ShapeDtypeStruct((bsz, RET_W), F32),
                   jax.ShapeDtypeStruct(state_all.shape, F32)],
        scratch_shapes=[pltpu.VMEM((bb, RET_W), F32)],
        input_output_aliases=aliases,
        compiler_params=_params(("parallel",)),
        name="ret_decode",
    )(qT, kT, zs, zs, cosc, sinc, lg_t, gn_g.reshape(1, RET_W), state_all, *lent)


def _hgrn_decode_body(q_ref, fT_ref, v_ref, g_ref, lbT_ref, ng_ref, s_ref, *rest, n_lent):
    o_ref, so_ref, orow = rest[n_lent:]
    for h in range(HG_HEADS):
        sl = slice(h * HG_DV, (h + 1) * HG_DV)
        qb = _silu(q_ref[:, sl]).astype(BF16)
        lb = lbT_ref[h]
        f = lb + (1.0 - lb) * _sigmoid(fT_ref[h])
        for j in range(DEC_BB):
            vrow = v_ref[j:j + 1, sl]
            s1 = f[:, j:j + 1] * (s_ref[j, h] - vrow) + vrow
            so_ref[j, h] = s1
            oj = jnp.dot(qb, s1.astype(BF16), preferred_element_type=F32)
            orow[j:j + 1, sl] = oj[j:j + 1, :]
    for h in range(HG_HEADS):
        sl = slice(h * HG_DV, (h + 1) * HG_DV)
        o = orow[:, sl]
        on = o * lax.rsqrt(jnp.mean(o * o, axis=-1, keepdims=True) + EPS) * ng_ref[:, sl]
        o_ref[:, sl] = on * _silu(g_ref[:, sl])


def _hgrn_decode(zs, state_all, layer, lb, ng, so_all):
    bsz = zs.shape[0]
    bb = DEC_BB
    fT = _cols(zs, OFF_FH, HG_HEADS, HG_DK)
    col_spec = pl.BlockSpec((None, HG_HEADS, HG_DK, bb), lambda i: (i, 0, 0, 0))
    st_spec = pl.BlockSpec((None, bb, HG_HEADS, HG_DK, HG_DV), lambda i: (layer, i, 0, 0, 0))
    in_specs = [pl.BlockSpec((bb, HG_W), lambda i: (i, OFF_QH // HG_W)),
                col_spec,
                pl.BlockSpec((bb, HG_W), lambda i: (i, OFF_IH // HG_W)),
                pl.BlockSpec((bb, HG_W), lambda i: (i, OFF_GH // HG_W)),
                pl.BlockSpec((HG_HEADS, HG_DK, 1), lambda i: (0, 0, 0)),
                pl.BlockSpec((1, HG_W), lambda i: (0, 0)),
                st_spec]
    lent_specs, aliases, lent = _lend(len(in_specs), [so_all], [1])
    return pl.pallas_call(
        functools.partial(_hgrn_decode_body, n_lent=len(lent)),
        grid=(bsz // bb,),
        in_specs=in_specs + lent_specs,
        out_specs=[pl.BlockSpec((bb, HG_W), lambda i: (i, 0)), st_spec],
        out_shape=[jax.ShapeDtypeStruct((bsz, HG_W), F32),
                   jax.ShapeDtypeStruct(state_all.shape, F32)],
        scratch_shapes=[pltpu.VMEM((bb, HG_W), F32)],
        input_output_aliases=aliases,
        compiler_params=_params(("parallel",)),
        name="hgrn_decode",
    )(zs, fT, zs, zs, lb.reshape(HG_HEADS, HG_DK, 1), ng.reshape(1, HG_W), state_all, *lent)


def _conv_decode_body(z7_ref, z8_ref, wdw_ref, bdw_ref, ln_g_ref, ln_b_ref, wpw_ref, bpw_ref, s_ref,
                      *rest, n_lent):
    o_ref, so_ref = rest[n_lent:]
    kh = CONV_K - 1
    a = z7_ref[:, 0:CONV_W]
    bgate = jnp.concatenate([z7_ref[:, CONV_W:_CBLK], z8_ref[:, 0:2 * CONV_W - _CBLK]], axis=-1)
    g = z8_ref[:, 2 * CONV_W - _CBLK:]
    u = a * _sigmoid(bgate)
    buf = s_ref[...]
    y = jnp.sum(buf * wdw_ref[0:kh, :][None], axis=1) + u * wdw_ref[kh:kh + 1, :] + bdw_ref[...]
    o_ref[...] = _conv_tail(y, g, ln_g_ref, ln_b_ref, wpw_ref, bpw_ref)
    so_ref[:, 0:kh - 1, :] = s_ref[:, 1:kh, :]
    for j in range(DEC_BB):
        so_ref[j, kh - 1:kh, :] = u[j:j + 1, :]


def _conv_decode(zs, state_all, layer, w_dw, b_dw, ln_g, ln_b, wpw_all, b_pw, so_all):
    bsz = zs.shape[0]
    bb = DEC_BB
    vec = lambda i: (0, 0)
    st_spec = pl.BlockSpec((None, bb, CONV_K - 1, CONV_W), lambda i: (layer, i, 0, 0))
    in_specs = [pl.BlockSpec((bb, _CBLK), lambda i: (i, OFF_AC // _CBLK)),
                pl.BlockSpec((bb, _CBLK), lambda i: (i, OFF_AC // _CBLK + 1)),
                pl.BlockSpec((CONV_K, CONV_W), vec),
                pl.BlockSpec((1, CONV_W), vec),
                pl.BlockSpec((1, CONV_W), vec),
                pl.BlockSpec((1, CONV_W), vec),
                pl.BlockSpec((None, CONV_W, CONV_W), lambda i: (layer, 0, 0)),
                pl.BlockSpec((1, CONV_W), vec),
                st_spec]
    lent_specs, aliases, lent = _lend(len(in_specs), [so_all], [1])
    return pl.pallas_call(
        functools.partial(_conv_decode_body, n_lent=len(lent)),
        grid=(bsz // bb,),
        in_specs=in_specs + lent_specs,
        out_specs=[pl.BlockSpec((bb, CONV_W), lambda i: (i, 0)), st_spec],
        out_shape=[jax.ShapeDtypeStruct((bsz, CONV_W), F32),
                   jax.ShapeDtypeStruct(state_all.shape, F32)],
        input_output_aliases=aliases,
        compiler_params=_params(("parallel",)),
        name="conv_decode",
    )(zs, zs, w_dw, b_dw.reshape(1, -1), ln_g.reshape(1, -1), ln_b.reshape(1, -1), wpw_all,
      b_pw.reshape(1, -1), state_all, *lent)


_N_DEC_IN = 16


def _decode_one_sequence(ins, outs, scr):
    (rq_ref, rk_ref, rv_ref, rg_ref, cos_ref, sin_ref, lg_ref, gn_ref, rs_ref,
     hq_ref, hf_ref, hv_ref, hg_ref, lbT_ref, ng_ref, hs_ref) = ins
    ro_ref, rso_ref, ho_ref, hso_ref = outs
    rrow, hrow, tsc = scr
    half = RET_DK // 2
    cosc = cos_ref[...]
    sinc = sin_ref[...]
    for h in range(RET_HEADS):
        tsc[h:h + 1, :] = rq_ref[:, h * RET_DK:(h + 1) * RET_DK]
        tsc[RET_HEADS + h:RET_HEADS + h + 1, :] = rk_ref[:, h * RET_DK:(h + 1) * RET_DK]
    for h in range(HG_HEADS):
        tsc[2 * RET_HEADS + h:2 * RET_HEADS + h + 1, :] = hf_ref[:, h * HG_DK:(h + 1) * HG_DK]
    cols = tsc[...].T
    for h in range(RET_HEADS):
        sl = slice(h * RET_DV, (h + 1) * RET_DV)
        zq = cols[:, h:h + 1]
        zk = cols[:, RET_HEADS + h:RET_HEADS + h + 1]
        q = zq * cosc + jnp.concatenate([zq[half:], zq[:half]], axis=0) * sinc
        k = (zk * cosc + jnp.concatenate([zk[half:], zk[:half]], axis=0) * sinc) * (RET_DK ** -0.5)
        s1 = jnp.exp(lg_ref[h:h + 1, :]) * rs_ref[h] + k * rv_ref[:, sl]
        rso_ref[h] = s1
        rrow[0:1, sl] = jnp.sum(q * s1, axis=0, keepdims=True)
    for h in range(RET_HEADS):
        sl = slice(h * RET_DV, (h + 1) * RET_DV)
        o = rrow[0:1, sl]
        mu = jnp.mean(o, axis=-1, keepdims=True)
        var = jnp.mean(jnp.square(o - mu), axis=-1, keepdims=True)
        on = (o - mu) * lax.rsqrt(var + EPS) * gn_ref[:, sl]
        ro_ref[:, sl] = on * _silu(rg_ref[:, sl])
    for h in range(HG_HEADS):
        sl = slice(h * HG_DV, (h + 1) * HG_DV)
        qb = jnp.broadcast_to(_silu(hq_ref[:, sl]), (8, HG_DK)).astype(BF16)
        lb = lbT_ref[h]
        zf = cols[:, 2 * RET_HEADS + h:2 * RET_HEADS + h + 1]
        f = lb + (1.0 - lb) * _sigmoid(zf)
        vrow = hv_ref[:, sl]
        s1 = f * (hs_ref[h] - vrow) + vrow
        hso_ref[h] = s1
        hrow[0:1, sl] = jnp.dot(qb, s1.astype(BF16), preferred_element_type=F32)[0:1, :]
    for h in range(HG_HEADS):
        sl = slice(h * HG_DV, (h + 1) * HG_DV)
        o = hrow[0:1, sl]
        on = o * lax.rsqrt(jnp.mean(o * o, axis=-1, keepdims=True) + EPS) * ng_ref[:, sl]
        ho_ref[:, sl] = on * _silu(hg_ref[:, sl])


def _inproj_decode_body(x_ref, w_ref, *rest, n_lent):
    dec_in = rest[:_N_DEC_IN]
    z_ref, ro_ref, rso_ref, ho_ref, hso_ref, acc_ref, rrow, hrow, tsc = rest[_N_DEC_IN + n_lent:]
    k = pl.program_id(2)
    first = jnp.logical_and(jnp.logical_and(pl.program_id(0) == 0, pl.program_id(1) == 0), k == 0)

    @pl.when(first)
    def _():
        tsc[...] = jnp.zeros_like(tsc)

    @pl.when(k == 0)
    def _():
        acc_ref[...] = jnp.dot(x_ref[...], w_ref[...], preferred_element_type=F32)
        _decode_one_sequence(dec_in, (ro_ref, rso_ref, ho_ref, hso_ref), (rrow, hrow, tsc))

    @pl.when(k == 1)
    def _():
        z_ref[...] = acc_ref[...] + jnp.dot(x_ref[...], w_ref[...], preferred_element_type=F32)
        _decode_one_sequence(dec_in, (ro_ref, rso_ref, ho_ref, hso_ref), (rrow, hrow, tsc))


def _inproj_with_decode(x, w_b, zs, state_ret, state_hgrn, layer, cos1, sin1, lg_t, gn_g, lb, ng,
                        rso_all, hso_all, *, tm, tn, tk):
    m, kdim = x.shape
    n = w_b.shape[-1]
    bsz = zs.shape[0]
    nj, nk = n // tn, kdim // tk
    assert nk == 2 and (m // tm) * nj * nk >= bsz
    seq = lambda i, j, k: jnp.minimum((i * nj + j) * nk + k, bsz - 1)
    z3 = zs.reshape(bsz, 1, IN_W)

    def row_spec(width, off):
        return pl.BlockSpec((None, 1, width), lambda i, j, k: (seq(i, j, k), 0, off // width))

    def state_spec(heads, dk, dv):
        return pl.BlockSpec((None, None, heads, dk, dv), lambda i, j, k: (layer, seq(i, j, k), 0, 0, 0))

    const2 = lambda i, j, k: (0, 0)
    in_specs = [pl.BlockSpec((tm, tk), lambda i, j, k: (i, k)),
                pl.BlockSpec((tk, tn), lambda i, j, k: (k, j)),
                row_spec(RET_HEADS * RET_DK, OFF_QR), row_spec(RET_HEADS * RET_DK, OFF_KR),
                row_spec(RET_W, OFF_VR), row_spec(RET_W, OFF_GR),
                pl.BlockSpec((RET_DK, 1), const2), pl.BlockSpec((RET_DK, 1), const2),
                pl.BlockSpec((RET_HEADS, RET_DV), const2), pl.BlockSpec((1, RET_W), const2),
                state_spec(RET_HEADS, RET_DK, RET_DV),
                row_spec(HG_W, OFF_QH), row_spec(HG_W, OFF_FH),
                row_spec(HG_W, OFF_IH), row_spec(HG_W, OFF_GH),
                pl.BlockSpec((HG_HEADS, HG_DK, 1), lambda i, j, k: (0, 0, 0)),
                pl.BlockSpec((1, HG_W), const2),
                state_spec(HG_HEADS, HG_DK, HG_DV)]
    args = [x, w_b,
            z3, z3, z3, z3,
            cos1, sin1, lg_t, gn_g.reshape(1, RET_W), state_ret,
            z3, z3, z3, z3,
            lb.reshape(HG_HEADS, HG_DK, 1), ng.reshape(1, HG_W), state_hgrn]
    assert len(in_specs) == 2 + _N_DEC_IN
    lent_specs, aliases, lent = _lend(len(in_specs), [rso_all, hso_all], [2, 4])
    orow_spec = lambda width: pl.BlockSpec((None, 1, width), lambda i, j, k: (seq(i, j, k), 0, 0))
    outs = pl.pallas_call(
        functools.partial(_inproj_decode_body, n_lent=len(lent)),
        grid=(m // tm, nj, nk),
        in_specs=in_specs + lent_specs,
        out_specs=[pl.BlockSpec((tm, tn), lambda i, j, k: (i, j)),
                   orow_spec(RET_W), state_spec(RET_HEADS, RET_DK, RET_DV),
                   orow_spec(HG_W), state_spec(HG_HEADS, HG_DK, HG_DV)],
        out_shape=[jax.ShapeDtypeStruct((m, n), F32),
                   jax.ShapeDtypeStruct((bsz, 1, RET_W), F32),
                   jax.ShapeDtypeStruct(state_ret.shape, F32),
                   jax.ShapeDtypeStruct((bsz, 1, HG_W), F32),
                   jax.ShapeDtypeStruct(state_hgrn.shape, F32)],
        scratch_shapes=[pltpu.VMEM((tm, tn), F32), pltpu.VMEM((8, RET_W), F32), pltpu.VMEM((8, HG_W), F32),
                        pltpu.VMEM((RET_DK, RET_DK), F32)],
        input_output_aliases=aliases,
        compiler_params=_params(("arbitrary", "arbitrary", "arbitrary")),
        name="inproj_decode",
    )(*args, *lent)
    z, ro, rso, ho, hso = outs
    return z, ro.reshape(bsz, RET_W), rso, ho.reshape(bsz, HG_W), hso


def _rope_tables(pos):
    half = RET_DK // 2
    inv = ROPE_BASE ** (-jnp.arange(half, dtype=F32) / half)
    ang = pos[:, None] * inv[None, :]
    cos = jnp.cos(ang)
    sin = jnp.sin(ang)
    return jnp.concatenate([cos, cos], axis=-1), jnp.concatenate([-sin, sin], axis=-1)


PROMPT_TILES_IN = (1024, 1536, 2048)
PROMPT_TILES_SQ = (1024, 512, D_MODEL)
SAMPLE_TN_IN = 768
SAMPLE_TN_SQ = 512


def kernel(x_prompt, x_sample, state_ret, state_hgrn, state_conv, p_prompt, p_sample, norm_g, w_in,
           ret_gn_g, hg_lower_bounds, hg_norm_g, w_dw, b_dw, conv_ln_g, conv_ln_b, w_pw, b_pw, w_out,
           w_ple, w_pg, final_norm_g):
    bp, tp, d = x_prompt.shape
    bs, ts, _ = x_sample.shape
    mp, ms = bp * tp, bs * ts
    assert ts == 1 and tp % CONV_TT == 0 and bs % DEC_BB == 0

    w_ple_b = w_ple.astype(BF16)
    w_pw_b = w_pw.astype(BF16)
    pp_b = p_prompt.reshape(DEPTH, mp, PLE_DIM).astype(BF16)
    ps_b = p_sample.reshape(DEPTH, ms, PLE_DIM).astype(BF16)

    lbs = jax.nn.softmax(hg_lower_bounds.astype(F32), axis=0)
    lbs = jnp.cumsum(lbs, axis=0) - lbs[0:1]
    log_gamma = np.log(1.0 - 2.0 ** (-5.0 - np.arange(RET_HEADS, dtype=np.float32))).astype(np.float32)
    lg_t = jnp.asarray(np.broadcast_to(log_gamma[:, None], (RET_HEADS, RET_DV)))

    cos_p, sin_p = _rope_tables(jnp.arange(tp, dtype=F32))
    cos_s, sin_s = _rope_tables(PAST_LEN + jnp.arange(ts, dtype=F32))
    cosc = jnp.broadcast_to(cos_s.reshape(RET_DK, 1), (RET_DK, DEC_BB))
    sinc = jnp.broadcast_to(sin_s.reshape(RET_DK, 1), (RET_DK, DEC_BB))

    cos1, sin1 = cos_s.reshape(RET_DK, 1), sin_s.reshape(RET_DK, 1)

    def layer(hp, hs, l, st_p, st_s):
        pr, ph, pc = st_p
        sr, sh, sc = st_s
        xs = _rmsnorm(hs, norm_g[l], BF16, ms)
        zs, w_in_b = _mm(xs, w_in, l, mode="plain", tm=ms, tn=SAMPLE_TN_IN, tk=d)

        tm, tn, tk = PROMPT_TILES_IN
        xp = _rmsnorm(hp, norm_g[l], BF16, 256)
        zp, mix_r, sr, mix_h, sh = _inproj_with_decode(
            xp, w_in_b, zs, state_ret, state_hgrn, l, cos1, sin1, lg_t, ret_gn_g[l], lbs[l],
            hg_norm_g[l], sr, sh, tm=tm, tn=tn, tk=tk)

        mix_c, sc = _conv_decode(zs, state_conv, l, w_dw[l], b_dw[l], conv_ln_g[l], conv_ln_b[l],
                                 w_pw_b, b_pw[l], sc)
        mix_s = jnp.concatenate([mix_r, mix_h, mix_c], axis=-1).astype(BF16)
        h1, h1b, w_out_b = _mm(mix_s, w_out, l, mode="resid", tm=ms, tn=SAMPLE_TN_SQ, tk=d, resid=hs)
        hs, w_pg_b = _mm(h1b, w_pg, l, mode="ple", tm=ms, tn=SAMPLE_TN_SQ, tk=d, resid=h1, p=ps_b,
                         wp_all=w_ple_b)

        mix, pr = _ret_prefill(zp, cos_p, sin_p, lg_t, ret_gn_g[l], bp, tp, l, pr)
        mix, ph = _hgrn_prefill(zp, mix, lbs[l], hg_norm_g[l], bp, tp, l, ph)
        mix, pc = _conv_prefill(zp, mix, w_dw[l], b_dw[l], conv_ln_g[l], conv_ln_b[l], w_pw_b,
                                b_pw[l], bp, tp, l, pc)
        tm, tn, tk = PROMPT_TILES_SQ
        h1, h1b = _mm(mix, w_out_b, l, mode="resid", tm=tm, tn=tn, tk=tk, resid=hp)
        (hp,) = _mm(h1b, w_pg_b, l, mode="ple", tm=tm, tn=tn, tk=tk, resid=h1, p=pp_b, wp_all=w_ple_b)
        return hp, hs, (pr, ph, pc), (sr, sh, sc)

    hp, hs = x_prompt.reshape(mp, d), x_sample.reshape(ms, d)
    st_p = st_s = (None, None, None)
    for l in range(DEPTH):
        hp, hs, st_p, st_s = layer(hp, hs, l, st_p, st_s)
    y_p = _rmsnorm(hp, final_norm_g, F32, 256).reshape(x_prompt.shape)
    y_s = _rmsnorm(hs, final_norm_g, F32, ms).reshape(x_sample.shape)
    return (y_p, y_s) + st_p + st_s
```

```python
import functools
import math

import numpy as np
import jax
import jax.numpy as jnp
from jax import lax
from jax.experimental import pallas as pl
from jax.experimental.pallas import tpu as pltpu

F32 = jnp.float32
BF16 = jnp.bfloat16

D_MODEL = 4096
DEPTH = 4
PAST_LEN = 16384
PLE_DIM = 256
RET_HEADS = 6
RET_DK = 128
RET_DV = 256
RET_W = RET_HEADS * RET_DV
ROPE_BASE = 10000.0
HG_HEADS = 12
HG_DK = 128
HG_DV = 128
HG_W = HG_HEADS * HG_DV
CONV_W = D_MODEL - RET_W - HG_W
CONV_K = 31
EPS = 1e-6
IN_W = 2 * RET_HEADS * RET_DK + 2 * RET_W + 2 * HG_HEADS * HG_DK + 2 * HG_W + 3 * CONV_W

OFF_QR = 0
OFF_KR = OFF_QR + RET_HEADS * RET_DK
OFF_VR = OFF_KR + RET_HEADS * RET_DK
OFF_GR = OFF_VR + RET_W
OFF_QH = OFF_GR + RET_W
OFF_FH = OFF_QH + HG_HEADS * HG_DK
OFF_IH = OFF_FH + HG_HEADS * HG_DK
OFF_GH = OFF_IH + HG_W
OFF_AC = OFF_GH + HG_W
OFF_BC = OFF_AC + CONV_W
OFF_GC = OFF_BC + CONV_W

V7X_VMEM_LIMIT_BYTES = 56 * 1024 * 1024
CHUNK = 128
HIST = 32
DEC_BB = 8

_NT = (((1,), (1,)), ((), ()))
_ANY = pl.BlockSpec(memory_space=pl.ANY)


def _params(sem):
    return pltpu.CompilerParams(dimension_semantics=sem, vmem_limit_bytes=V7X_VMEM_LIMIT_BYTES)


def _sigmoid(x):
    return 0.5 * jnp.tanh(0.5 * x) + 0.5


def _silu(x):
    return x * _sigmoid(x)


def _lend(n_in, carried, out_positions):
    specs, aliases = [], {}
    for buf, out_pos in zip(carried, out_positions):
        if buf is not None:
            aliases[n_in + len(specs)] = out_pos
            specs.append(_ANY)
    return specs, aliases, [b for b in carried if b is not None]


def _rmsnorm_body(x_ref, g_ref, o_ref):
    x = x_ref[...]
    ms = jnp.mean(x * x, axis=-1, keepdims=True)
    o_ref[...] = (x * lax.rsqrt(ms + EPS) * g_ref[...]).astype(o_ref.dtype)


def _rmsnorm(x, g, out_dtype, tm):
    m, d = x.shape
    return pl.pallas_call(
        _rmsnorm_body,
        grid=(m // tm,),
        in_specs=[pl.BlockSpec((tm, d), lambda i: (i, 0)),
                  pl.BlockSpec((1, d), lambda i: (0, 0))],
        out_specs=pl.BlockSpec((tm, d), lambda i: (i, 0)),
        out_shape=jax.ShapeDtypeStruct((m, d), out_dtype),
        compiler_params=_params(("parallel",)),
        name="rmsnorm",
    )(x, g.reshape(1, d))


def _mm_body(x_ref, w_ref, *rest, nk, mode, cast_w, cast_next):
    if nk > 1:
        acc_ref = rest[-1]
        rest = rest[:-1]
    if cast_next:
        nxt_ref, nxtb_ref = rest[0], rest[-1]
        rest = rest[1:-1]
    if cast_w:
        wb_ref = rest[-1]
        rest = rest[:-1]

    def partial_product():
        if cast_next:
            nxtb_ref[...] = nxt_ref[...].astype(BF16)
        w = w_ref[...]
        if cast_w:
            w = w.astype(BF16)
            wb_ref[...] = w
        return jnp.dot(x_ref[...], w, preferred_element_type=F32)

    def epilogue(acc):
        if mode == "plain":
            (o_ref,) = rest
            o_ref[...] = acc
        elif mode == "resid":
            r_ref, o_ref, ob_ref = rest
            h1 = r_ref[...] + acc
            o_ref[...] = h1
            ob_ref[...] = h1.astype(BF16)
        else:
            r_ref, p_ref, wp_ref, o_ref = rest
            ple = jnp.dot(p_ref[...], wp_ref[...], preferred_element_type=F32)
            o_ref[...] = r_ref[...] + _sigmoid(acc) * ple

    if nk == 1:
        epilogue(partial_product())
        return
    k = pl.program_id(2)

    @pl.when(k == 0)
    def _():
        acc_ref[...] = partial_product()

    if nk > 2:
        @pl.when(jnp.logical_and(k > 0, k < nk - 1))
        def _():
            acc_ref[...] += partial_product()

    @pl.when(k == nk - 1)
    def _():
        epilogue(acc_ref[...] + partial_product())


def _mm(x, w, layer, *, mode, tm, tn, tk, resid=None, p=None, wp_all=None, next_w=None):
    m, kdim = x.shape
    n = w.shape[-1]
    nk = kdim // tk
    cast_w = w.ndim == 3
    assert not cast_w or (nk == 1 and m == tm)
    grid = (m // tm, n // tn, nk)
    w_spec = (pl.BlockSpec((None, tk, tn), lambda i, j, k: (layer, k, j)) if cast_w
              else pl.BlockSpec((tk, tn), lambda i, j, k: (k, j)))
    in_specs = [pl.BlockSpec((tm, tk), lambda i, j, k: (i, k)), w_spec]
    args = [x, w]
    if next_w is not None:
        n_steps = grid[0] * grid[1]
        slab = next_w.shape[1] // n_steps
        assert nk == 1 and slab * n_steps == next_w.shape[1] and slab % 16 == 0
        in_specs.append(pl.BlockSpec((None, slab, next_w.shape[2]),
                                     lambda i, j, k: (layer + 1, i * grid[1] + j, 0)))
        args.append(next_w)
    o_spec = pl.BlockSpec((tm, tn), lambda i, j, k: (i, j))
    out_specs = [o_spec]
    out_shape = [jax.ShapeDtypeStruct((m, n), F32)]
    if mode == "resid":
        in_specs.append(o_spec)
        args.append(resid)
        out_specs.append(o_spec)
        out_shape.append(jax.ShapeDtypeStruct((m, n), BF16))
    elif mode == "ple":
        pdim = p.shape[-1]
        in_specs += [o_spec,
                     pl.BlockSpec((None, tm, pdim), lambda i, j, k: (layer, i, 0)),
                     pl.BlockSpec((None, pdim, tn), lambda i, j, k: (layer, 0, j))]
        args += [resid, p, wp_all]
    if cast_w:
        out_specs.append(pl.BlockSpec((tk, tn), lambda i, j, k: (k, j)))
        out_shape.append(jax.ShapeDtypeStruct((kdim, n), BF16))
    if next_w is not None:
        out_specs.append(pl.BlockSpec((slab, next_w.shape[2]), lambda i, j, k: (i * grid[1] + j, 0)))
        out_shape.append(jax.ShapeDtypeStruct(next_w.shape[1:], BF16))
    return pl.pallas_call(
        functools.partial(_mm_body, nk=nk, mode=mode, cast_w=cast_w, cast_next=next_w is not None),
        grid=grid,
        in_specs=in_specs,
        out_specs=out_specs,
        out_shape=out_shape,
        scratch_shapes=[pltpu.VMEM((tm, tn), F32)] if nk > 1 else [],
        compiler_params=_params(("parallel", "parallel", "arbitrary")),
        name="mm_" + mode + ("_cast" if cast_w else ""),
    )(*args)


def _ret_prefill_body(q_ref, k_ref, v_ref, g_ref, cos_ref, sin_ref, lg_ref, gn_ref, *rest, n_lent):
    o_ref, s_ref, st_scr = rest[n_lent:]
    c = pl.program_id(1)
    nc = pl.num_programs(1)
    L = CHUNK

    @pl.when(c == 0)
    def _():
        st_scr[...] = jnp.zeros_like(st_scr)

    cos = cos_ref[...]
    sin = sin_ref[...]
    row = lax.broadcasted_iota(jnp.int32, (L, L), 0)
    col = lax.broadcasted_iota(jnp.int32, (L, L), 1)
    diff = (row - col).astype(F32)
    causal = row >= col
    rowf = lax.broadcasted_iota(jnp.int32, (L, RET_DK), 0).astype(F32)
    for h in range(RET_HEADS):
        lg = lg_ref[h:h + 1, :]
        lg1 = lg[:, :RET_DK]
        decay_in = jnp.where(causal, jnp.exp(jnp.where(causal, diff, 0.0) * lg1), 0.0)
        q_decay = jnp.exp((rowf + 1.0) * lg1)
        k_decay = jnp.exp((L - 1.0 - rowf) * lg1)
        chunk_decay = jnp.exp(L * lg)

        zq = q_ref[:, h * RET_DK:(h + 1) * RET_DK]
        zk = k_ref[:, h * RET_DK:(h + 1) * RET_DK]
        v = v_ref[:, h * RET_DV:(h + 1) * RET_DV]
        zg = g_ref[:, h * RET_DV:(h + 1) * RET_DV]
        q = zq * cos + pltpu.roll(zq, RET_DK // 2, 1) * sin
        k = (zk * cos + pltpu.roll(zk, RET_DK // 2, 1) * sin) * (RET_DK ** -0.5)
        vb = v.astype(BF16)
        s0 = st_scr[h]

        scores = lax.dot_general(q.astype(BF16), k.astype(BF16), _NT,
                                 preferred_element_type=F32) * decay_in
        o = (jnp.dot(scores.astype(BF16), vb, preferred_element_type=F32)
             + jnp.dot((q * q_decay).astype(BF16), s0.astype(BF16), preferred_element_type=F32))
        kT = (k * k_decay).T.astype(BF16)
        st_scr[h] = s0 * chunk_decay + jnp.dot(kT, vb, preferred_element_type=F32)

        mu = jnp.mean(o, axis=-1, keepdims=True)
        var = jnp.mean(jnp.square(o - mu), axis=-1, keepdims=True)
        on = (o - mu) * lax.rsqrt(var + EPS) * gn_ref[:, h * RET_DV:(h + 1) * RET_DV]
        o_ref[:, h * RET_DV:(h + 1) * RET_DV] = (on * _silu(zg)).astype(o_ref.dtype)

    @pl.when(c == nc - 1)
    def _():
        s_ref[0] = st_scr[...]


def _ret_prefill(z, cos_t, sin_t, lg_t, gn_g, batch, seq, layer, s_all):
    m = z.shape[0]
    nc = seq // CHUNK
    L = CHUNK
    rowmap = lambda blk: (lambda b, c: (b * nc + c, blk))
    in_specs = [pl.BlockSpec((L, RET_HEADS * RET_DK), rowmap(OFF_QR // (RET_HEADS * RET_DK))),
                pl.BlockSpec((L, RET_HEADS * RET_DK), rowmap(OFF_KR // (RET_HEADS * RET_DK))),
                pl.BlockSpec((L, RET_W), rowmap(OFF_VR // RET_W)),
                pl.BlockSpec((L, RET_W), rowmap(OFF_GR // RET_W)),
                pl.BlockSpec((L, RET_DK), lambda b, c: (c, 0)),
                pl.BlockSpec((L, RET_DK), lambda b, c: (c, 0)),
                pl.BlockSpec((RET_HEADS, RET_DV), lambda b, c: (0, 0)),
                pl.BlockSpec((1, RET_W), lambda b, c: (0, 0))]
    lent_specs, aliases, lent = _lend(len(in_specs), [s_all], [1])
    return pl.pallas_call(
        functools.partial(_ret_prefill_body, n_lent=len(lent)),
        grid=(batch, nc),
        in_specs=in_specs + lent_specs,
        out_specs=[pl.BlockSpec((L, RET_W), lambda b, c: (b * nc + c, 0)),
                   pl.BlockSpec((None, 1, RET_HEADS, RET_DK, RET_DV), lambda b, c: (layer, b, 0, 0, 0))],
        out_shape=[jax.ShapeDtypeStruct((m, D_MODEL), BF16),
                   jax.ShapeDtypeStruct((DEPTH, batch, RET_HEADS, RET_DK, RET_DV), F32)],
        scratch_shapes=[pltpu.VMEM((RET_HEADS, RET_DK, RET_DV), F32)],
        input_output_aliases=aliases,
        compiler_params=_params(("parallel", "arbitrary")),
        name="ret_prefill",
    )(z, z, z, z, cos_t, sin_t, lg_t, gn_g.reshape(1, RET_W), *lent)


HG_HB = 12
_NLEV = int(math.log2(CHUNK))


def _level_table():
    i = np.arange(CHUNK)[:, None]
    j = np.arange(CHUNK)[None, :]
    x = np.bitwise_xor(i, j)
    lvl = np.floor(np.log2(np.maximum(x, 1))).astype(np.int32)
    lvl = np.where(i == j, _NLEV, lvl)
    lvl = np.where(i < j, -1, lvl)
    return lvl.astype(np.int32)


def _split3(x):
    x1 = x.astype(BF16)
    r1 = x - x1.astype(F32)
    x2 = r1.astype(BF16)
    x3 = (r1 - x2.astype(F32)).astype(BF16)
    return x1, x2, x3


def _ref_rows(b, m):
    L, w = b.shape
    g = 2 * m
    b3 = b.reshape(L // g, g, w)
    r = jnp.broadcast_to(b3[:, m - 1:m, :], b3.shape)
    return r.reshape(L, w)


def _hgrn_prefill_body(q_ref, f_ref, i_ref, g_ref, lb_ref, ng_ref, tri_ref, lvl_ref, *rest, n_lent):
    o_ref, s_ref, st_scr = rest[n_lent:]
    c = pl.program_id(2)
    nc = pl.num_programs(2)
    L = CHUNK

    @pl.when(c == 0)
    def _():
        st_scr[...] = jnp.zeros_like(st_scr)

    lvl = lvl_ref[...]
    rowi = lax.broadcasted_iota(jnp.int32, (L, HG_DK), 0)
    odd = (rowi & 1) == 1
    r4 = rowi & 3
    r4_is0, r4_is1, r4_is2 = r4 == 0, r4 == 1, r4 == 2

    lb_all = lb_ref[...]
    f_all = lb_all + (1.0 - lb_all) * _sigmoid(f_ref[...])
    l1, l2, l3 = _split3(jnp.log(f_all))
    b_parts = jnp.dot(tri_ref[...], jnp.concatenate([l1, l2, l3], axis=1), preferred_element_type=F32)
    wblk = HG_HB * HG_DK
    b_all = b_parts[:, 0:wblk] + b_parts[:, wblk:2 * wblk] + b_parts[:, 2 * wblk:]
    for hh in range(HG_HB):
        sl = slice(hh * HG_DK, (hh + 1) * HG_DK)
        v = i_ref[:, sl]
        zg = g_ref[:, sl]
        q = _silu(q_ref[:, sl])
        f = f_all[:, sl]
        k = 1.0 - f
        b = b_all[:, sl]
        qb = q.astype(BF16)
        kb = k.astype(BF16)
        vb = v.astype(BF16)

        a = jnp.where(lvl == _NLEV, lax.dot_general(qb, kb, _NT, preferred_element_type=F32), 0.0)
        f_prev = pltpu.roll(f, 1, 0)
        f_next = pltpu.roll(f, L - 1, 0)
        for t in range(_NLEV):
            m = 1 << t
            if m == 1:
                e = jnp.where(odd, f, 1.0)
            elif m == 2:
                e = jnp.where(r4_is0, f_next, jnp.where(r4_is1, 1.0, jnp.where(r4_is2, f, f * f_prev)))
            else:
                e = jnp.exp(-jnp.abs(b - _ref_rows(b, m)))
            eb = e.astype(BF16)
            p = lax.dot_general(qb * eb, kb * eb, _NT, preferred_element_type=F32)
            a = jnp.where(lvl == t, p, a)

        st = st_scr[hh]
        o = (jnp.dot(a.astype(BF16), vb, preferred_element_type=F32)
             + lax.dot_general((q * jnp.exp(b)).astype(BF16), st.astype(BF16), _NT,
                               preferred_element_type=F32))
        b_last = b[L - 1:L, :]
        k_s = (k * jnp.exp(b_last - b)).astype(BF16)
        st_scr[hh] = st * jnp.exp(b_last) + jnp.dot(v.T.astype(BF16), k_s, preferred_element_type=F32)

        on = o * lax.rsqrt(jnp.mean(o * o, axis=-1, keepdims=True) + EPS) * ng_ref[:, sl]
        o_ref[:, sl] = (on * _silu(zg)).astype(o_ref.dtype)

    @pl.when(c == nc - 1)
    def _():
        for hh in range(HG_HB):
            s_ref[0, hh] = st_scr[hh].T


def _hgrn_prefill(z, mix, lb, ng, batch, seq, layer, s_all):
    nc = seq // CHUNK
    L = CHUNK
    ng_groups = HG_HEADS // HG_HB
    wblk = HG_HB * HG_DK
    rowmap = lambda off: (lambda b, g, c: (b * nc + c, off // wblk + g))
    tri = jnp.asarray(np.tril(np.ones((L, L), np.float32)), dtype=BF16)
    lvl = jnp.asarray(_level_table())
    in_specs = [pl.BlockSpec((L, wblk), rowmap(OFF_QH)),
                pl.BlockSpec((L, wblk), rowmap(OFF_FH)),
                pl.BlockSpec((L, wblk), rowmap(OFF_IH)),
                pl.BlockSpec((L, wblk), rowmap(OFF_GH)),
                pl.BlockSpec((1, wblk), lambda b, g, c: (0, g)),
                pl.BlockSpec((1, wblk), lambda b, g, c: (0, g)),
                pl.BlockSpec((L, L), lambda b, g, c: (0, 0)),
                pl.BlockSpec((L, L), lambda b, g, c: (0, 0))]
    lent_specs, aliases, lent = _lend(len(in_specs), [mix, s_all], [0, 1])
    return pl.pallas_call(
        functools.partial(_hgrn_prefill_body, n_lent=len(lent)),
        grid=(batch, ng_groups, nc),
        in_specs=in_specs + lent_specs,
        out_specs=[pl.BlockSpec((L, wblk), rowmap(RET_W)),
                   pl.BlockSpec((None, 1, HG_HB, HG_DK, HG_DV), lambda b, g, c: (layer, b, g, 0, 0))],
        out_shape=[jax.ShapeDtypeStruct(mix.shape, BF16),
                   jax.ShapeDtypeStruct((DEPTH, batch, HG_HEADS, HG_DK, HG_DV), F32)],
        scratch_shapes=[pltpu.VMEM((HG_HB, HG_DV, HG_DK), F32)],
        input_output_aliases=aliases,
        compiler_params=_params(("parallel", "parallel", "arbitrary")),
        name="hgrn_prefill",
    )(z, z, z, z, lb.reshape(1, HG_W), ng.reshape(1, HG_W), tri, lvl, *lent)


CONV_TT = 256
CONV_RC = 32
_CBLK = 1536


def _conv_tail(y, g, ln_g_ref, ln_b_ref, wpw_ref, bpw_ref):
    mu = jnp.mean(y, axis=-1, keepdims=True)
    var = jnp.mean(jnp.square(y - mu), axis=-1, keepdims=True)
    yn = (y - mu) * lax.rsqrt(var + EPS) * ln_g_ref[...] + ln_b_ref[...]
    pw = jnp.dot(_silu(yn).astype(BF16), wpw_ref[...], preferred_element_type=F32) + bpw_ref[...]
    return pw * _silu(g)


def _conv_prefill_body(z7_ref, z8_ref, wdw_ref, bdw_ref, ln_g_ref, ln_b_ref, wpw_ref, bpw_ref,
                       *rest, n_lent):
    o_ref, s_ref, ext, shifted, ybuf = rest[n_lent:]
    t = pl.program_id(1)
    nt = pl.num_programs(1)
    tt = CONV_TT
    sub = 8

    @pl.when(t == 0)
    def _():
        ext[0:HIST, :] = jnp.zeros((HIST, CONV_W), F32)

    a = z7_ref[:, 0:CONV_W]
    bgate = jnp.concatenate([z7_ref[:, CONV_W:_CBLK], z8_ref[:, 0:2 * CONV_W - _CBLK]], axis=-1)
    g = z8_ref[:, 2 * CONV_W - _CBLK:]
    ext[HIST:HIST + tt, :] = a * _sigmoid(bgate)

    base = HIST - (CONV_K - 1)
    n_sh = HIST + tt - sub
    for r in range(1, sub):
        shifted[r, 0:n_sh, :] = ext[r:r + n_sh, :]
    for ci in range(tt // CONV_RC):
        r0 = ci * CONV_RC
        acc = jnp.zeros((CONV_RC // sub, sub, CONV_W), F32)
        for k in range(CONV_K):
            r = (base + k) % sub
            a0 = r0 + (base + k) - r
            rows = ext[a0:a0 + CONV_RC, :] if r == 0 else shifted[r, a0:a0 + CONV_RC, :]
            acc = acc + wdw_ref[k][None] * rows.reshape(CONV_RC // sub, sub, CONV_W)
        ybuf[r0:r0 + CONV_RC, :] = acc.reshape(CONV_RC, CONV_W)
    y = ybuf[...] + bdw_ref[...]
    o_ref[...] = _conv_tail(y, g, ln_g_ref, ln_b_ref, wpw_ref, bpw_ref).astype(o_ref.dtype)

    @pl.when(t == nt - 1)
    def _():
        s_ref[0] = ext[tt + HIST - (CONV_K - 1):tt + HIST, :]

    ext[0:HIST, :] = ext[tt:tt + HIST, :]


def _conv_prefill(z, mix, w_dw, b_dw, ln_g, ln_b, wpw_all, b_pw, batch, seq, layer, s_all):
    tt = CONV_TT
    nt = seq // tt
    vec = lambda i, t: (0, 0)
    in_specs = [pl.BlockSpec((tt, _CBLK), lambda b, t: (b * nt + t, OFF_AC // _CBLK)),
                pl.BlockSpec((tt, _CBLK), lambda b, t: (b * nt + t, OFF_AC // _CBLK + 1)),
                pl.BlockSpec((CONV_K, 8, CONV_W), lambda b, t: (0, 0, 0)),
                pl.BlockSpec((1, CONV_W), vec),
                pl.BlockSpec((1, CONV_W), vec),
                pl.BlockSpec((1, CONV_W), vec),
                pl.BlockSpec((None, CONV_W, CONV_W), lambda b, t: (layer, 0, 0)),
                pl.BlockSpec((1, CONV_W), vec)]
    lent_specs, aliases, lent = _lend(len(in_specs), [mix, s_all], [0, 1])
    return pl.pallas_call(
        functools.partial(_conv_prefill_body, n_lent=len(lent)),
        grid=(batch, nt),
        in_specs=in_specs + lent_specs,
        out_specs=[pl.BlockSpec((tt, CONV_W), lambda b, t: (b * nt + t, (RET_W + HG_W) // CONV_W)),
                   pl.BlockSpec((None, 1, CONV_K - 1, CONV_W), lambda b, t: (layer, b, 0, 0))],
        out_shape=[jax.ShapeDtypeStruct(mix.shape, BF16),
                   jax.ShapeDtypeStruct((DEPTH, batch, CONV_K - 1, CONV_W), F32)],
        scratch_shapes=[pltpu.VMEM((HIST + tt, CONV_W), F32), pltpu.VMEM((8, HIST + tt, CONV_W), F32),
                        pltpu.VMEM((tt, CONV_W), F32)],
        input_output_aliases=aliases,
        compiler_params=_params(("parallel", "arbitrary")),
        name="conv_prefill",
    )(z, z, jnp.broadcast_to(w_dw[:, None, :], (CONV_K, 8, CONV_W)), b_dw.reshape(1, -1),
      ln_g.reshape(1, -1), ln_b.reshape(1, -1), wpw_all, b_pw.reshape(1, -1), *lent)


def _ret_decode_body(qT_ref, kT_ref, v_ref, g_ref, cos_ref, sin_ref, lg_ref, gn_ref, s_ref, *rest,
                     n_lent):
    o_ref, so_ref, orow = rest[n_lent:]
    half = RET_DK // 2
    cosc = cos_ref[...]
    sinc = sin_ref[...]
    for h in range(RET_HEADS):
        zq = qT_ref[h]
        zk = kT_ref[h]
        q = zq * cosc + jnp.concatenate([zq[half:], zq[:half]], axis=0) * sinc
        k = (zk * cosc + jnp.concatenate([zk[half:], zk[:half]], axis=0) * sinc) * (RET_DK ** -0.5)
        gamma = jnp.exp(lg_ref[h:h + 1, :])
        for j in range(DEC_BB):
            vrow = v_ref[j:j + 1, h * RET_DV:(h + 1) * RET_DV]
            s1 = gamma * s_ref[j, h] + k[:, j:j + 1] * vrow
            so_ref[j, h] = s1
            orow[j:j + 1, h * RET_DV:(h + 1) * RET_DV] = jnp.sum(q[:, j:j + 1] * s1, axis=0, keepdims=True)
    for h in range(RET_HEADS):
        sl = slice(h * RET_DV, (h + 1) * RET_DV)
        o = orow[:, sl]
        mu = jnp.mean(o, axis=-1, keepdims=True)
        var = jnp.mean(jnp.square(o - mu), axis=-1, keepdims=True)
        on = (o - mu) * lax.rsqrt(var + EPS) * gn_ref[:, sl]
        o_ref[:, sl] = on * _silu(g_ref[:, sl])


def _cols(zs, off, heads, dk):
    bsz = zs.shape[0]
    x = zs[:, off:off + heads * dk].reshape(bsz // DEC_BB, DEC_BB, heads, dk)
    return x.transpose(0, 2, 3, 1)


def _ret_decode(zs, state_all, layer, cosc, sinc, lg_t, gn_g, so_all):
    bsz = zs.shape[0]
    bb = DEC_BB
    qT = _cols(zs, OFF_QR, RET_HEADS, RET_DK)
    kT = _cols(zs, OFF_KR, RET_HEADS, RET_DK)
    col_spec = pl.BlockSpec((None, RET_HEADS, RET_DK, bb), lambda i: (i, 0, 0, 0))
    st_spec = pl.BlockSpec((None, bb, RET_HEADS, RET_DK, RET_DV), lambda i: (layer, i, 0, 0, 0))
    in_specs = [col_spec, col_spec,
                pl.BlockSpec((bb, RET_W), lambda i: (i, OFF_VR // RET_W)),
                pl.BlockSpec((bb, RET_W), lambda i: (i, OFF_GR // RET_W)),
                pl.BlockSpec((RET_DK, bb), lambda i: (0, 0)),
                pl.BlockSpec((RET_DK, bb), lambda i: (0, 0)),
                pl.BlockSpec((RET_HEADS, RET_DV), lambda i: (0, 0)),
                pl.BlockSpec((1, RET_W), lambda i: (0, 0)),
                st_spec]
    lent_specs, aliases, lent = _lend(len(in_specs), [so_all], [1])
    return pl.pallas_call(
        functools.partial(_ret_decode_body, n_lent=len(lent)),
        grid=(bsz // bb,),
        in_specs=in_specs + lent_specs,
        out_specs=[pl.BlockSpec((bb, RET_W), lambda i: (i, 0)), st_spec],
        out_shape=[jax.ShapeDtypeStruct((bsz, RET_W), F32),
                   jax.ShapeDtypeStruct(state_all.shape, F32)],
        scratch_shapes=[pltpu.VMEM((bb, RET_W), F32)],
        input_output_aliases=aliases,
        compiler_params=_params(("parallel",)),
        name="ret_decode",
    )(qT, kT, zs, zs, cosc, sinc, lg_t, gn_g.reshape(1, RET_W), state_all, *lent)


def _hgrn_decode_body(q_ref, fT_ref, v_ref, g_ref, lbT_ref, ng_ref, s_ref, *rest, n_lent):
    o_ref, so_ref, orow = rest[n_lent:]
    for h in range(HG_HEADS):
        sl = slice(h * HG_DV, (h + 1) * HG_DV)
        qb = _silu(q_ref[:, sl]).astype(BF16)
        lb = lbT_ref[h]
        f = lb + (1.0 - lb) * _sigmoid(fT_ref[h])
        for j in range(DEC_BB):
            vrow = v_ref[j:j + 1, sl]
            s1 = f[:, j:j + 1] * (s_ref[j, h] - vrow) + vrow
            so_ref[j, h] = s1
            oj = jnp.dot(qb, s1.astype(BF16), preferred_element_type=F32)
            orow[j:j + 1, sl] = oj[j:j + 1, :]
    for h in range(HG_HEADS):
        sl = slice(h * HG_DV, (h + 1) * HG_DV)
        o = orow[:, sl]
        on = o * lax.rsqrt(jnp.mean(o * o, axis=-1, keepdims=True) + EPS) * ng_ref[:, sl]
        o_ref[:, sl] = on * _silu(g_ref[:, sl])


def _hgrn_decode(zs, state_all, layer, lb, ng, so_all):
    bsz = zs.shape[0]
    bb = DEC_BB
    fT = _cols(zs, OFF_FH, HG_HEADS, HG_DK)
    col_spec = pl.BlockSpec((None, HG_HEADS, HG_DK, bb), lambda i: (i, 0, 0, 0))
    st_spec = pl.BlockSpec((None, bb, HG_HEADS, HG_DK, HG_DV), lambda i: (layer, i, 0, 0, 0))
    in_specs = [pl.BlockSpec((bb, HG_W), lambda i: (i, OFF_QH // HG_W)),
                col_spec,
                pl.BlockSpec((bb, HG_W), lambda i: (i, OFF_IH // HG_W)),
                pl.BlockSpec((bb, HG_W), lambda i: (i, OFF_GH // HG_W)),
                pl.BlockSpec((HG_HEADS, HG_DK, 1), lambda i: (0, 0, 0)),
                pl.BlockSpec((1, HG_W), lambda i: (0, 0)),
                st_spec]
    lent_specs, aliases, lent = _lend(len(in_specs), [so_all], [1])
    return pl.pallas_call(
        functools.partial(_hgrn_decode_body, n_lent=len(lent)),
        grid=(bsz // bb,),
        in_specs=in_specs + lent_specs,
        out_specs=[pl.BlockSpec((bb, HG_W), lambda i: (i, 0)), st_spec],
        out_shape=[jax.ShapeDtypeStruct((bsz, HG_W), F32),
                   jax.ShapeDtypeStruct(state_all.shape, F32)],
        scratch_shapes=[pltpu.VMEM((bb, HG_W), F32)],
        input_output_aliases=aliases,
        compiler_params=_params(("parallel",)),
        name="hgrn_decode",
    )(zs, fT, zs, zs, lb.reshape(HG_HEADS, HG_DK, 1), ng.reshape(1, HG_W), state_all, *lent)


def _conv_decode_body(z7_ref, z8_ref, wdw_ref, bdw_ref, ln_g_ref, ln_b_ref, wpw_ref, bpw_ref, s_ref,
                      *rest, n_lent):
    o_ref, so_ref = rest[n_lent:]
    kh = CONV_K - 1
    a = z7_ref[:, 0:CONV_W]
    bgate = jnp.concatenate([z7_ref[:, CONV_W:_CBLK], z8_ref[:, 0:2 * CONV_W - _CBLK]], axis=-1)
    g = z8_ref[:, 2 * CONV_W - _CBLK:]
    u = a * _sigmoid(bgate)
    buf = s_ref[...]
    y = jnp.sum(buf * wdw_ref[0:kh, :][None], axis=1) + u * wdw_ref[kh:kh + 1, :] + bdw_ref[...]
    o_ref[...] = _conv_tail(y, g, ln_g_ref, ln_b_ref, wpw_ref, bpw_ref)
    so_ref[:, 0:kh - 1, :] = s_ref[:, 1:kh, :]
    for j in range(DEC_BB):
        so_ref[j, kh - 1:kh, :] = u[j:j + 1, :]


def _conv_decode(zs, state_all, layer, w_dw, b_dw, ln_g, ln_b, wpw_all, b_pw, so_all):
    bsz = zs.shape[0]
    bb = DEC_BB
    vec = lambda i: (0, 0)
    st_spec = pl.BlockSpec((None, bb, CONV_K - 1, CONV_W), lambda i: (layer, i, 0, 0))
    in_specs = [pl.BlockSpec((bb, _CBLK), lambda i: (i, OFF_AC // _CBLK)),
                pl.BlockSpec((bb, _CBLK), lambda i: (i, OFF_AC // _CBLK + 1)),
                pl.BlockSpec((CONV_K, CONV_W), vec),
                pl.BlockSpec((1, CONV_W), vec),
                pl.BlockSpec((1, CONV_W), vec),
                pl.BlockSpec((1, CONV_W), vec),
                pl.BlockSpec((None, CONV_W, CONV_W), lambda i: (layer, 0, 0)),
                pl.BlockSpec((1, CONV_W), vec),
                st_spec]
    lent_specs, aliases, lent = _lend(len(in_specs), [so_all], [1])
    return pl.pallas_call(
        functools.partial(_conv_decode_body, n_lent=len(lent)),
        grid=(bsz // bb,),
        in_specs=in_specs + lent_specs,
        out_specs=[pl.BlockSpec((bb, CONV_W), lambda i: (i, 0)), st_spec],
        out_shape=[jax.ShapeDtypeStruct((bsz, CONV_W), F32),
                   jax.ShapeDtypeStruct(state_all.shape, F32)],
        input_output_aliases=aliases,
        compiler_params=_params(("parallel",)),
        name="conv_decode",
    )(zs, zs, w_dw, b_dw.reshape(1, -1), ln_g.reshape(1, -1), ln_b.reshape(1, -1), wpw_all,
      b_pw.reshape(1, -1), state_all, *lent)


_N_DEC_IN = 16


def _decode_one_sequence(ins, outs, scr):
    (rq_ref, rk_ref, rv_ref, rg_ref, cos_ref, sin_ref, lg_ref, gn_ref, rs_ref,
     hq_ref, hf_ref, hv_ref, hg_ref, lbT_ref, ng_ref, hs_ref) = ins
    ro_ref, rso_ref, ho_ref, hso_ref = outs
    rrow, hrow, tsc = scr
    half = RET_DK // 2
    cosc = cos_ref[...]
    sinc = sin_ref[...]
    for h in range(RET_HEADS):
        tsc[h:h + 1, :] = rq_ref[:, h * RET_DK:(h + 1) * RET_DK]
        tsc[RET_HEADS + h:RET_HEADS + h + 1, :] = rk_ref[:, h * RET_DK:(h + 1) * RET_DK]
    for h in range(HG_HEADS):
        tsc[2 * RET_HEADS + h:2 * RET_HEADS + h + 1, :] = hf_ref[:, h * HG_DK:(h + 1) * HG_DK]
    cols = tsc[...].T
    for h in range(RET_HEADS):
        sl = slice(h * RET_DV, (h + 1) * RET_DV)
        zq = cols[:, h:h + 1]
        zk = cols[:, RET_HEADS + h:RET_HEADS + h + 1]
        q = zq * cosc + jnp.concatenate([zq[half:], zq[:half]], axis=0) * sinc
        k = (zk * cosc + jnp.concatenate([zk[half:], zk[:half]], axis=0) * sinc) * (RET_DK ** -0.5)
        s1 = jnp.exp(lg_ref[h:h + 1, :]) * rs_ref[h] + k * rv_ref[:, sl]
        rso_ref[h] = s1
        rrow[0:1, sl] = jnp.sum(q * s1, axis=0, keepdims=True)
    for h in range(RET_HEADS):
        sl = slice(h * RET_DV, (h + 1) * RET_DV)
        o = rrow[0:1, sl]
        mu = jnp.mean(o, axis=-1, keepdims=True)
        var = jnp.mean(jnp.square(o - mu), axis=-1, keepdims=True)
        on = (o - mu) * lax.rsqrt(var + EPS) * gn_ref[:, sl]
        ro_ref[:, sl] = on * _silu(rg_ref[:, sl])
    for h in range(HG_HEADS):
        sl = slice(h * HG_DV, (h + 1) * HG_DV)
        qb = jnp.broadcast_to(_silu(hq_ref[:, sl]), (8, HG_DK)).astype(BF16)
        lb = lbT_ref[h]
        zf = cols[:, 2 * RET_HEADS + h:2 * RET_HEADS + h + 1]
        f = lb + (1.0 - lb) * _sigmoid(zf)
        vrow = hv_ref[:, sl]
        s1 = f * (hs_ref[h] - vrow) + vrow
        hso_ref[h] = s1
        hrow[0:1, sl] = jnp.dot(qb, s1.astype(BF16), preferred_element_type=F32)[0:1, :]
    for h in range(HG_HEADS):
        sl = slice(h * HG_DV, (h + 1) * HG_DV)
        o = hrow[0:1, sl]
        on = o * lax.rsqrt(jnp.mean(o * o, axis=-1, keepdims=True) + EPS) * ng_ref[:, sl]
        ho_ref[:, sl] = on * _silu(hg_ref[:, sl])


def _inproj_decode_body(x_ref, w_ref, *rest, n_lent, cast_next):
    dec_in = rest[:_N_DEC_IN]
    rest = rest[_N_DEC_IN:]
    if cast_next:
        nxt_ref = rest[0]
        z_ref, ro_ref, rso_ref, ho_ref, hso_ref, nxtb_ref, acc_ref, rrow, hrow, tsc = rest[1 + n_lent:]
    else:
        z_ref, ro_ref, rso_ref, ho_ref, hso_ref, acc_ref, rrow, hrow, tsc = rest[n_lent:]
    k = pl.program_id(2)

    def side_work():
        _decode_one_sequence(dec_in, (ro_ref, rso_ref, ho_ref, hso_ref), (rrow, hrow, tsc))
        if cast_next:
            nxtb_ref[...] = nxt_ref[...].astype(BF16)

    first = jnp.logical_and(jnp.logical_and(pl.program_id(0) == 0, pl.program_id(1) == 0), k == 0)

    @pl.when(first)
    def _():
        tsc[...] = jnp.zeros_like(tsc)

    @pl.when(k == 0)
    def _():
        acc_ref[...] = jnp.dot(x_ref[...], w_ref[...], preferred_element_type=F32)
        side_work()

    @pl.when(k == 1)
    def _():
        z_ref[...] = acc_ref[...] + jnp.dot(x_ref[...], w_ref[...], preferred_element_type=F32)
        side_work()


def _inproj_with_decode(x, w_b, zs, state_ret, state_hgrn, layer, cos1, sin1, lg_t, gn_g, lb, ng,
                        rso_all, hso_all, *, tm, tn, tk, next_w=None):
    m, kdim = x.shape
    n = w_b.shape[-1]
    bsz = zs.shape[0]
    nj, nk = n // tn, kdim // tk
    assert nk == 2 and (m // tm) * nj * nk >= bsz
    seq = lambda i, j, k: jnp.minimum((i * nj + j) * nk + k, bsz - 1)
    z3 = zs.reshape(bsz, 1, IN_W)

    def row_spec(width, off):
        return pl.BlockSpec((None, 1, width), lambda i, j, k: (seq(i, j, k), 0, off // width))

    def state_spec(heads, dk, dv):
        return pl.BlockSpec((None, None, heads, dk, dv), lambda i, j, k: (layer, seq(i, j, k), 0, 0, 0))

    const2 = lambda i, j, k: (0, 0)
    in_specs = [pl.BlockSpec((tm, tk), lambda i, j, k: (i, k)),
                pl.BlockSpec((tk, tn), lambda i, j, k: (k, j)),
                row_spec(RET_HEADS * RET_DK, OFF_QR), row_spec(RET_HEADS * RET_DK, OFF_KR),
                row_spec(RET_W, OFF_VR), row_spec(RET_W, OFF_GR),
                pl.BlockSpec((RET_DK, 1), const2), pl.BlockSpec((RET_DK, 1), const2),
                pl.BlockSpec((RET_HEADS, RET_DV), const2), pl.BlockSpec((1, RET_W), const2),
                state_spec(RET_HEADS, RET_DK, RET_DV),
                row_spec(HG_W, OFF_QH), row_spec(HG_W, OFF_FH),
                row_spec(HG_W, OFF_IH), row_spec(HG_W, OFF_GH),
                pl.BlockSpec((HG_HEADS, HG_DK, 1), lambda i, j, k: (0, 0, 0)),
                pl.BlockSpec((1, HG_W), const2),
                state_spec(HG_HEADS, HG_DK, HG_DV)]
    args = [x, w_b,
            z3, z3, z3, z3,
            cos1, sin1, lg_t, gn_g.reshape(1, RET_W), state_ret,
            z3, z3, z3, z3,
            lb.reshape(HG_HEADS, HG_DK, 1), ng.reshape(1, HG_W), state_hgrn]
    assert len(in_specs) == 2 + _N_DEC_IN
    out_specs = [pl.BlockSpec((tm, tn), lambda i, j, k: (i, j)),
                 pl.BlockSpec((None, 1, RET_W), lambda i, j, k: (seq(i, j, k), 0, 0)),
                 state_spec(RET_HEADS, RET_DK, RET_DV),
                 pl.BlockSpec((None, 1, HG_W), lambda i, j, k: (seq(i, j, k), 0, 0)),
                 state_spec(HG_HEADS, HG_DK, HG_DV)]
    out_shape = [jax.ShapeDtypeStruct((m, n), F32),
                 jax.ShapeDtypeStruct((bsz, 1, RET_W), F32),
                 jax.ShapeDtypeStruct(state_ret.shape, F32),
                 jax.ShapeDtypeStruct((bsz, 1, HG_W), F32),
                 jax.ShapeDtypeStruct(state_hgrn.shape, F32)]
    if next_w is not None:
        slab = next_w.shape[1] // bsz
        assert slab * bsz == next_w.shape[1] and slab % 16 == 0
        in_specs.append(pl.BlockSpec((None, slab, n), lambda i, j, k: (layer + 1, seq(i, j, k), 0)))
        args.append(next_w)
        out_specs.append(pl.BlockSpec((slab, n), lambda i, j, k: (seq(i, j, k), 0)))
        out_shape.append(jax.ShapeDtypeStruct(next_w.shape[1:], BF16))
    lent_specs, aliases, lent = _lend(len(in_specs), [rso_all, hso_all], [2, 4])
    outs = pl.pallas_call(
        functools.partial(_inproj_decode_body, n_lent=len(lent), cast_next=next_w is not None),
        grid=(m // tm, nj, nk),
        in_specs=in_specs + lent_specs,
        out_specs=out_specs,
        out_shape=out_shape,
        scratch_shapes=[pltpu.VMEM((tm, tn), F32), pltpu.VMEM((8, RET_W), F32), pltpu.VMEM((8, HG_W), F32),
                        pltpu.VMEM((RET_DK, RET_DK), F32)],
        input_output_aliases=aliases,
        compiler_params=_params(("arbitrary", "arbitrary", "arbitrary")),
        name="inproj_decode",
    )(*args, *lent)
    z, ro, rso, ho, hso = outs[:5]
    return (z, ro.reshape(bsz, RET_W), rso, ho.reshape(bsz, HG_W), hso) + tuple(outs[5:])


def _rope_tables(pos):
    half = RET_DK // 2
    inv = ROPE_BASE ** (-jnp.arange(half, dtype=F32) / half)
    ang = pos[:, None] * inv[None, :]
    cos = jnp.cos(ang)
    sin = jnp.sin(ang)
    return jnp.concatenate([cos, cos], axis=-1), jnp.concatenate([-sin, sin], axis=-1)


PROMPT_TILES_IN = (1024, 1536, 2048)
PROMPT_TILES_SQ = (1024, 512, D_MODEL)
SAMPLE_TN_IN = 768
SAMPLE_TN_SQ = 512


def kernel(x_prompt, x_sample, state_ret, state_hgrn, state_conv, p_prompt, p_sample, norm_g, w_in,
           ret_gn_g, hg_lower_bounds, hg_norm_g, w_dw, b_dw, conv_ln_g, conv_ln_b, w_pw, b_pw, w_out,
           w_ple, w_pg, final_norm_g):
    bp, tp, d = x_prompt.shape
    bs, ts, _ = x_sample.shape
    mp, ms = bp * tp, bs * ts
    assert ts == 1 and tp % CONV_TT == 0 and bs % DEC_BB == 0

    w_ple_b = w_ple.astype(BF16)
    w_pw_b = w_pw.astype(BF16)
    pp_b = p_prompt.reshape(DEPTH, mp, PLE_DIM).astype(BF16)
    ps_b = p_sample.reshape(DEPTH, ms, PLE_DIM).astype(BF16)

    lbs = jax.nn.softmax(hg_lower_bounds.astype(F32), axis=0)
    lbs = jnp.cumsum(lbs, axis=0) - lbs[0:1]
    log_gamma = np.log(1.0 - 2.0 ** (-5.0 - np.arange(RET_HEADS, dtype=np.float32))).astype(np.float32)
    lg_t = jnp.asarray(np.broadcast_to(log_gamma[:, None], (RET_HEADS, RET_DV)))

    cos_p, sin_p = _rope_tables(jnp.arange(tp, dtype=F32))
    cos_s, sin_s = _rope_tables(PAST_LEN + jnp.arange(ts, dtype=F32))
    cosc = jnp.broadcast_to(cos_s.reshape(RET_DK, 1), (RET_DK, DEC_BB))
    sinc = jnp.broadcast_to(sin_s.reshape(RET_DK, 1), (RET_DK, DEC_BB))

    cos1, sin1 = cos_s.reshape(RET_DK, 1), sin_s.reshape(RET_DK, 1)

    def layer(hp, hs, l, st_p, st_s, weights_b):
        pr, ph, pc = st_p
        sr, sh, sc = st_s
        w_in_b, w_out_b, w_pg_b = weights_b
        nxt = l + 1 < DEPTH
        xs = _rmsnorm(hs, norm_g[l], BF16, ms)
        if w_in_b is None:
            zs, w_in_b = _mm(xs, w_in, l, mode="plain", tm=ms, tn=SAMPLE_TN_IN, tk=d)
        else:
            (zs,) = _mm(xs, w_in_b, l, mode="plain", tm=ms, tn=SAMPLE_TN_IN, tk=d)

        tm, tn, tk = PROMPT_TILES_IN
        xp = _rmsnorm(hp, norm_g[l], BF16, 256)
        zp, mix_r, sr, mix_h, sh, *w_in_next = _inproj_with_decode(
            xp, w_in_b, zs, state_ret, state_hgrn, l, cos1, sin1, lg_t, ret_gn_g[l], lbs[l],
            hg_norm_g[l], sr, sh, tm=tm, tn=tn, tk=tk, next_w=w_in if nxt else None)

        mix_c, sc = _conv_decode(zs, state_conv, l, w_dw[l], b_dw[l], conv_ln_g[l], conv_ln_b[l],
                                 w_pw_b, b_pw[l], sc)
        mix_s = jnp.concatenate([mix_r, mix_h, mix_c], axis=-1).astype(BF16)
        if w_out_b is None:
            h1, h1b, w_out_b = _mm(mix_s, w_out, l, mode="resid", tm=ms, tn=SAMPLE_TN_SQ, tk=d, resid=hs)
            hs, w_pg_b = _mm(h1b, w_pg, l, mode="ple", tm=ms, tn=SAMPLE_TN_SQ, tk=d, resid=h1, p=ps_b,
                             wp_all=w_ple_b)
        else:
            h1, h1b = _mm(mix_s, w_out_b, l, mode="resid", tm=ms, tn=SAMPLE_TN_SQ, tk=d, resid=hs)
            (hs,) = _mm(h1b, w_pg_b, l, mode="ple", tm=ms, tn=SAMPLE_TN_SQ, tk=d, resid=h1, p=ps_b,
                        wp_all=w_ple_b)

        mix, pr = _ret_prefill(zp, cos_p, sin_p, lg_t, ret_gn_g[l], bp, tp, l, pr)
        mix, ph = _hgrn_prefill(zp, mix, lbs[l], hg_norm_g[l], bp, tp, l, ph)
        mix, pc = _conv_prefill(zp, mix, w_dw[l], b_dw[l], conv_ln_g[l], conv_ln_b[l], w_pw_b,
                                b_pw[l], bp, tp, l, pc)
        tm, tn, tk = PROMPT_TILES_SQ
        h1, h1b, *w_out_next = _mm(mix, w_out_b, l, mode="resid", tm=tm, tn=tn, tk=tk, resid=hp,
                                   next_w=w_out if nxt else None)
        hp, *w_pg_next = _mm(h1b, w_pg_b, l, mode="ple", tm=tm, tn=tn, tk=tk, resid=h1, p=pp_b,
                             wp_all=w_ple_b, next_w=w_pg if nxt else None)
        weights_next = (w_in_next[0], w_out_next[0], w_pg_next[0]) if nxt else None
        return hp, hs, (pr, ph, pc), (sr, sh, sc), weights_next

    hp, hs = x_prompt.reshape(mp, d), x_sample.reshape(ms, d)
    st_p = st_s = (None, None, None)
    weights_b = (None, None, None)
    for l in range(DEPTH):
        hp, hs, st_p, st_s, weights_b = layer(hp, hs, l, st_p, st_s, weights_b)
    y_p = _rmsnorm(hp, final_norm_g, F32, 256).reshape(x_prompt.shape)
    y_s = _rmsnorm(hs, final_norm_g, F32, ms).reshape(x_sample.shape)
    return (y_p, y_s) + st_p + st_s
```

```python
import functools
import math

import numpy as np
import jax
import jax.numpy as jnp
from jax import lax
from jax.experimental import pallas as pl
from jax.experimental.pallas import tpu as pltpu

F32 = jnp.float32
BF16 = jnp.bfloat16

D_MODEL = 4096
DEPTH = 4
PAST_LEN = 16384
PLE_DIM = 256
RET_HEADS = 6
RET_DK = 128
RET_DV = 256
RET_W = RET_HEADS * RET_DV
ROPE_BASE = 10000.0
HG_HEADS = 12
HG_DK = 128
HG_DV = 128
HG_W = HG_HEADS * HG_DV
CONV_W = D_MODEL - RET_W - HG_W
CONV_K = 31
EPS = 1e-6
IN_W = 2 * RET_HEADS * RET_DK + 2 * RET_W + 2 * HG_HEADS * HG_DK + 2 * HG_W + 3 * CONV_W

OFF_QR = 0
OFF_KR = OFF_QR + RET_HEADS * RET_DK
OFF_VR = OFF_KR + RET_HEADS * RET_DK
OFF_GR = OFF_VR + RET_W
OFF_QH = OFF_GR + RET_W
OFF_FH = OFF_QH + HG_HEADS * HG_DK
OFF_IH = OFF_FH + HG_HEADS * HG_DK
OFF_GH = OFF_IH + HG_W
OFF_AC = OFF_GH + HG_W
OFF_BC = OFF_AC + CONV_W
OFF_GC = OFF_BC + CONV_W

V7X_VMEM_LIMIT_BYTES = 56 * 1024 * 1024
CHUNK = 128
HIST = 32
DEC_BB = 8

_NT = (((1,), (1,)), ((), ()))
_ANY = pl.BlockSpec(memory_space=pl.ANY)


def _params(sem):
    return pltpu.CompilerParams(dimension_semantics=sem, vmem_limit_bytes=V7X_VMEM_LIMIT_BYTES)


def _sigmoid(x):
    return 0.5 * jnp.tanh(0.5 * x) + 0.5


def _silu(x):
    return x * _sigmoid(x)


def _lend(n_in, carried, out_positions):
    specs, aliases = [], {}
    for buf, out_pos in zip(carried, out_positions):
        if buf is not None:
            aliases[n_in + len(specs)] = out_pos
            specs.append(_ANY)
    return specs, aliases, [b for b in carried if b is not None]


def _rmsnorm_body(x_ref, g_ref, o_ref):
    x = x_ref[...]
    ms = jnp.mean(x * x, axis=-1, keepdims=True)
    o_ref[...] = (x * lax.rsqrt(ms + EPS) * g_ref[...]).astype(o_ref.dtype)


def _rmsnorm(x, g, out_dtype, tm):
    m, d = x.shape
    return pl.pallas_call(
        _rmsnorm_body,
        grid=(m // tm,),
        in_specs=[pl.BlockSpec((tm, d), lambda i: (i, 0)),
                  pl.BlockSpec((1, d), lambda i: (0, 0))],
        out_specs=pl.BlockSpec((tm, d), lambda i: (i, 0)),
        out_shape=jax.ShapeDtypeStruct((m, d), out_dtype),
        compiler_params=_params(("parallel",)),
        name="rmsnorm",
    )(x, g.reshape(1, d))


def _mm_body(x_ref, w_ref, *rest, nk, mode, cast_w, cast_next):
    if nk > 1:
        acc_ref = rest[-1]
        rest = rest[:-1]
    if cast_next:
        nxt_ref, nxtb_ref = rest[0], rest[-1]
        rest = rest[1:-1]
    if cast_w:
        wb_ref = rest[-1]
        rest = rest[:-1]

    def partial_product():
        if cast_next:
            nxtb_ref[...] = nxt_ref[...].astype(BF16)
        w = w_ref[...]
        if cast_w:
            w = w.astype(BF16)
            wb_ref[...] = w
        return jnp.dot(x_ref[...], w, preferred_element_type=F32)

    def epilogue(acc):
        if mode == "plain":
            (o_ref,) = rest
            o_ref[...] = acc
        elif mode == "resid":
            r_ref, o_ref, ob_ref = rest
            h1 = r_ref[...] + acc
            o_ref[...] = h1
            ob_ref[...] = h1.astype(BF16)
        else:
            r_ref, p_ref, wp_ref, o_ref = rest
            ple = jnp.dot(p_ref[...], wp_ref[...], preferred_element_type=F32)
            o_ref[...] = r_ref[...] + _sigmoid(acc) * ple

    if nk == 1:
        epilogue(partial_product())
        return
    k = pl.program_id(2)

    @pl.when(k == 0)
    def _():
        acc_ref[...] = partial_product()

    if nk > 2:
        @pl.when(jnp.logical_and(k > 0, k < nk - 1))
        def _():
            acc_ref[...] += partial_product()

    @pl.when(k == nk - 1)
    def _():
        epilogue(acc_ref[...] + partial_product())


def _mm(x, w, layer, *, mode, tm, tn, tk, resid=None, p=None, wp_all=None, next_w=None):
    m, kdim = x.shape
    n = w.shape[-1]
    nk = kdim // tk
    cast_w = w.ndim == 3
    assert not cast_w or (nk == 1 and m == tm)
    grid = (m // tm, n // tn, nk)
    w_spec = (pl.BlockSpec((None, tk, tn), lambda i, j, k: (layer, k, j)) if cast_w
              else pl.BlockSpec((tk, tn), lambda i, j, k: (k, j)))
    in_specs = [pl.BlockSpec((tm, tk), lambda i, j, k: (i, k)), w_spec]
    args = [x, w]
    if next_w is not None:
        n_steps = grid[0] * grid[1]
        slab = next_w.shape[1] // n_steps
        assert nk == 1 and slab * n_steps == next_w.shape[1] and slab % 16 == 0
        in_specs.append(pl.BlockSpec((None, slab, next_w.shape[2]),
                                     lambda i, j, k: (layer + 1, i * grid[1] + j, 0)))
        args.append(next_w)
    o_spec = pl.BlockSpec((tm, tn), lambda i, j, k: (i, j))
    out_specs = [o_spec]
    out_shape = [jax.ShapeDtypeStruct((m, n), F32)]
    if mode == "resid":
        in_specs.append(o_spec)
        args.append(resid)
        out_specs.append(o_spec)
        out_shape.append(jax.ShapeDtypeStruct((m, n), BF16))
    elif mode == "ple":
        pdim = p.shape[-1]
        in_specs += [o_spec,
                     pl.BlockSpec((None, tm, pdim), lambda i, j, k: (layer, i, 0)),
                     pl.BlockSpec((None, pdim, tn), lambda i, j, k: (layer, 0, j))]
        args += [resid, p, wp_all]
    if cast_w:
        out_specs.append(pl.BlockSpec((tk, tn), lambda i, j, k: (k, j)))
        out_shape.append(jax.ShapeDtypeStruct((kdim, n), BF16))
    if next_w is not None:
        out_specs.append(pl.BlockSpec((slab, next_w.shape[2]), lambda i, j, k: (i * grid[1] + j, 0)))
        out_shape.append(jax.ShapeDtypeStruct(next_w.shape[1:], BF16))
    return pl.pallas_call(
        functools.partial(_mm_body, nk=nk, mode=mode, cast_w=cast_w, cast_next=next_w is not None),
        grid=grid,
        in_specs=in_specs,
        out_specs=out_specs,
        out_shape=out_shape,
        scratch_shapes=[pltpu.VMEM((tm, tn), F32)] if nk > 1 else [],
        compiler_params=_params(("parallel", "parallel", "arbitrary")),
        name="mm_" + mode + ("_cast" if cast_w else ""),
    )(*args)


def _ret_prefill_body(q_ref, k_ref, v_ref, g_ref, cos_ref, sin_ref, lg_ref, gn_ref, *rest, n_lent):
    o_ref, s_ref, st_scr = rest[n_lent:]
    c = pl.program_id(1)
    nc = pl.num_programs(1)
    L = CHUNK

    @pl.when(c == 0)
    def _():
        st_scr[...] = jnp.zeros_like(st_scr)

    cos = cos_ref[...]
    sin = sin_ref[...]
    row = lax.broadcasted_iota(jnp.int32, (L, L), 0)
    col = lax.broadcasted_iota(jnp.int32, (L, L), 1)
    diff = (row - col).astype(F32)
    causal = row >= col
    rowf = lax.broadcasted_iota(jnp.int32, (L, RET_DK), 0).astype(F32)
    for h in range(RET_HEADS):
        lg = lg_ref[h:h + 1, :]
        lg1 = lg[:, :RET_DK]
        decay_in = jnp.where(causal, jnp.exp(jnp.where(causal, diff, 0.0) * lg1), 0.0)
        q_decay = jnp.exp((rowf + 1.0) * lg1)
        k_decay = jnp.exp((L - 1.0 - rowf) * lg1)
        chunk_decay = jnp.exp(L * lg)

        zq = q_ref[:, h * RET_DK:(h + 1) * RET_DK]
        zk = k_ref[:, h * RET_DK:(h + 1) * RET_DK]
        v = v_ref[:, h * RET_DV:(h + 1) * RET_DV]
        zg = g_ref[:, h * RET_DV:(h + 1) * RET_DV]
        q = zq * cos + pltpu.roll(zq, RET_DK // 2, 1) * sin
        k = (zk * cos + pltpu.roll(zk, RET_DK // 2, 1) * sin) * (RET_DK ** -0.5)
        vb = v.astype(BF16)
        s0 = st_scr[h]

        scores = lax.dot_general(q.astype(BF16), k.astype(BF16), _NT,
                                 preferred_element_type=F32) * decay_in
        o = (jnp.dot(scores.astype(BF16), vb, preferred_element_type=F32)
             + jnp.dot((q * q_decay).astype(BF16), s0.astype(BF16), preferred_element_type=F32))
        kT = (k * k_decay).T.astype(BF16)
        st_scr[h] = s0 * chunk_decay + jnp.dot(kT, vb, preferred_element_type=F32)

        mu = jnp.mean(o, axis=-1, keepdims=True)
        var = jnp.mean(jnp.square(o - mu), axis=-1, keepdims=True)
        on = (o - mu) * lax.rsqrt(var + EPS) * gn_ref[:, h * RET_DV:(h + 1) * RET_DV]
        o_ref[:, h * RET_DV:(h + 1) * RET_DV] = (on * _silu(zg)).astype(o_ref.dtype)

    @pl.when(c == nc - 1)
    def _():
        s_ref[0] = st_scr[...]


def _ret_prefill(z, cos_t, sin_t, lg_t, gn_g, batch, seq, layer, s_all):
    m = z.shape[0]
    nc = seq // CHUNK
    L = CHUNK
    rowmap = lambda blk: (lambda b, c: (b * nc + c, blk))
    in_specs = [pl.BlockSpec((L, RET_HEADS * RET_DK), rowmap(OFF_QR // (RET_HEADS * RET_DK))),
                pl.BlockSpec((L, RET_HEADS * RET_DK), rowmap(OFF_KR // (RET_HEADS * RET_DK))),
                pl.BlockSpec((L, RET_W), rowmap(OFF_VR // RET_W)),
                pl.BlockSpec((L, RET_W), rowmap(OFF_GR // RET_W)),
                pl.BlockSpec((L, RET_DK), lambda b, c: (c, 0)),
                pl.BlockSpec((L, RET_DK), lambda b, c: (c, 0)),
                pl.BlockSpec((RET_HEADS, RET_DV), lambda b, c: (0, 0)),
                pl.BlockSpec((1, RET_W), lambda b, c: (0, 0))]
    lent_specs, aliases, lent = _lend(len(in_specs), [s_all], [1])
    return pl.pallas_call(
        functools.partial(_ret_prefill_body, n_lent=len(lent)),
        grid=(batch, nc),
        in_specs=in_specs + lent_specs,
        out_specs=[pl.BlockSpec((L, RET_W), lambda b, c: (b * nc + c, 0)),
                   pl.BlockSpec((None, 1, RET_HEADS, RET_DK, RET_DV), lambda b, c: (layer, b, 0, 0, 0))],
        out_shape=[jax.ShapeDtypeStruct((m, D_MODEL), BF16),
                   jax.ShapeDtypeStruct((DEPTH, batch, RET_HEADS, RET_DK, RET_DV), F32)],
        scratch_shapes=[pltpu.VMEM((RET_HEADS, RET_DK, RET_DV), F32)],
        input_output_aliases=aliases,
        compiler_params=_params(("parallel", "arbitrary")),
        name="ret_prefill",
    )(z, z, z, z, cos_t, sin_t, lg_t, gn_g.reshape(1, RET_W), *lent)


HG_HB = 12
_NLEV = int(math.log2(CHUNK))


def _level_table():
    i = np.arange(CHUNK)[:, None]
    j = np.arange(CHUNK)[None, :]
    x = np.bitwise_xor(i, j)
    lvl = np.floor(np.log2(np.maximum(x, 1))).astype(np.int32)
    lvl = np.where(i == j, _NLEV, lvl)
    lvl = np.where(i < j, -1, lvl)
    return lvl.astype(np.int32)


def _split3(x):
    x1 = x.astype(BF16)
    r1 = x - x1.astype(F32)
    x2 = r1.astype(BF16)
    x3 = (r1 - x2.astype(F32)).astype(BF16)
    return x1, x2, x3


def _ref_rows(b, m):
    L, w = b.shape
    g = 2 * m
    b3 = b.reshape(L // g, g, w)
    r = jnp.broadcast_to(b3[:, m - 1:m, :], b3.shape)
    return r.reshape(L, w)


def _hgrn_prefill_body(q_ref, f_ref, i_ref, g_ref, lb_ref, ng_ref, tri_ref, lvl_ref, *rest, n_lent,
                       cast_next):
    if cast_next:
        o_ref, s_ref, nxtb_ref, st_scr = rest[1 + n_lent:]
        nxtb_ref[...] = rest[0][...].astype(BF16)
    else:
        o_ref, s_ref, st_scr = rest[n_lent:]
    c = pl.program_id(2)
    nc = pl.num_programs(2)
    L = CHUNK

    @pl.when(c == 0)
    def _():
        st_scr[...] = jnp.zeros_like(st_scr)

    lvl = lvl_ref[...]
    rowi = lax.broadcasted_iota(jnp.int32, (L, HG_DK), 0)
    odd = (rowi & 1) == 1
    r4 = rowi & 3
    r4_is0, r4_is1, r4_is2 = r4 == 0, r4 == 1, r4 == 2

    lb_all = lb_ref[...]
    f_all = lb_all + (1.0 - lb_all) * _sigmoid(f_ref[...])
    l1, l2, l3 = _split3(jnp.log(f_all))
    b_parts = jnp.dot(tri_ref[...], jnp.concatenate([l1, l2, l3], axis=1), preferred_element_type=F32)
    wblk = HG_HB * HG_DK
    b_all = b_parts[:, 0:wblk] + b_parts[:, wblk:2 * wblk] + b_parts[:, 2 * wblk:]
    for hh in range(HG_HB):
        sl = slice(hh * HG_DK, (hh + 1) * HG_DK)
        v = i_ref[:, sl]
        zg = g_ref[:, sl]
        q = _silu(q_ref[:, sl])
        f = f_all[:, sl]
        k = 1.0 - f
        b = b_all[:, sl]
        qb = q.astype(BF16)
        kb = k.astype(BF16)
        vb = v.astype(BF16)

        a = jnp.where(lvl == _NLEV, lax.dot_general(qb, kb, _NT, preferred_element_type=F32), 0.0)
        f_prev = pltpu.roll(f, 1, 0)
        f_next = pltpu.roll(f, L - 1, 0)
        for t in range(_NLEV):
            m = 1 << t
            if m == 1:
                e = jnp.where(odd, f, 1.0)
            elif m == 2:
                e = jnp.where(r4_is0, f_next, jnp.where(r4_is1, 1.0, jnp.where(r4_is2, f, f * f_prev)))
            else:
                e = jnp.exp(-jnp.abs(b - _ref_rows(b, m)))
            eb = e.astype(BF16)
            p = lax.dot_general(qb * eb, kb * eb, _NT, preferred_element_type=F32)
            a = jnp.where(lvl == t, p, a)

        st = st_scr[hh]
        o = (jnp.dot(a.astype(BF16), vb, preferred_element_type=F32)
             + lax.dot_general((q * jnp.exp(b)).astype(BF16), st.astype(BF16), _NT,
                               preferred_element_type=F32))
        b_last = b[L - 1:L, :]
        k_s = (k * jnp.exp(b_last - b)).astype(BF16)
        st_scr[hh] = st * jnp.exp(b_last) + jnp.dot(v.T.astype(BF16), k_s, preferred_element_type=F32)

        on = o * lax.rsqrt(jnp.mean(o * o, axis=-1, keepdims=True) + EPS) * ng_ref[:, sl]
        o_ref[:, sl] = (on * _silu(zg)).astype(o_ref.dtype)

    @pl.when(c == nc - 1)
    def _():
        for hh in range(HG_HB):
            s_ref[0, hh] = st_scr[hh].T


def _hgrn_prefill(z, mix, lb, ng, batch, seq, layer, s_all, next_w=None):
    nc = seq // CHUNK
    L = CHUNK
    ng_groups = HG_HEADS // HG_HB
    wblk = HG_HB * HG_DK
    rowmap = lambda off: (lambda b, g, c: (b * nc + c, off // wblk + g))
    tri = jnp.asarray(np.tril(np.ones((L, L), np.float32)), dtype=BF16)
    lvl = jnp.asarray(_level_table())
    in_specs = [pl.BlockSpec((L, wblk), rowmap(OFF_QH)),
                pl.BlockSpec((L, wblk), rowmap(OFF_FH)),
                pl.BlockSpec((L, wblk), rowmap(OFF_IH)),
                pl.BlockSpec((L, wblk), rowmap(OFF_GH)),
                pl.BlockSpec((1, wblk), lambda b, g, c: (0, g)),
                pl.BlockSpec((1, wblk), lambda b, g, c: (0, g)),
                pl.BlockSpec((L, L), lambda b, g, c: (0, 0)),
                pl.BlockSpec((L, L), lambda b, g, c: (0, 0))]
    args = [z, z, z, z, lb.reshape(1, HG_W), ng.reshape(1, HG_W), tri, lvl]
    out_specs = [pl.BlockSpec((L, wblk), rowmap(RET_W)),
                 pl.BlockSpec((None, 1, HG_HB, HG_DK, HG_DV), lambda b, g, c: (layer, b, g, 0, 0))]
    out_shape = [jax.ShapeDtypeStruct(mix.shape, BF16),
                 jax.ShapeDtypeStruct((DEPTH, batch, HG_HEADS, HG_DK, HG_DV), F32)]
    if next_w is not None:
        n_steps = batch * ng_groups * nc
        slab = next_w.shape[1] // n_steps
        assert slab * n_steps == next_w.shape[1] and slab % 16 == 0
        step = lambda b, g, c: (b * ng_groups + g) * nc + c
        in_specs.append(pl.BlockSpec((None, slab, next_w.shape[2]),
                                     lambda b, g, c: (layer + 1, step(b, g, c), 0)))
        args.append(next_w)
        out_specs.append(pl.BlockSpec((slab, next_w.shape[2]), lambda b, g, c: (step(b, g, c), 0)))
        out_shape.append(jax.ShapeDtypeStruct(next_w.shape[1:], BF16))
    lent_specs, aliases, lent = _lend(len(in_specs), [mix, s_all], [0, 1])
    return pl.pallas_call(
        functools.partial(_hgrn_prefill_body, n_lent=len(lent), cast_next=next_w is not None),
        grid=(batch, ng_groups, nc),
        in_specs=in_specs + lent_specs,
        out_specs=out_specs,
        out_shape=out_shape,
        scratch_shapes=[pltpu.VMEM((HG_HB, HG_DV, HG_DK), F32)],
        input_output_aliases=aliases,
        compiler_params=_params(("parallel", "parallel", "arbitrary")),
        name="hgrn_prefill",
    )(*args, *lent)


CONV_TT = 256
CONV_RC = 32
_CBLK = 1536


def _conv_tail(y, g, ln_g_ref, ln_b_ref, wpw_ref, bpw_ref):
    mu = jnp.mean(y, axis=-1, keepdims=True)
    var = jnp.mean(jnp.square(y - mu), axis=-1, keepdims=True)
    yn = (y - mu) * lax.rsqrt(var + EPS) * ln_g_ref[...] + ln_b_ref[...]
    pw = jnp.dot(_silu(yn).astype(BF16), wpw_ref[...], preferred_element_type=F32) + bpw_ref[...]
    return pw * _silu(g)


def _conv_prefill_body(z7_ref, z8_ref, wdw_ref, bdw_ref, ln_g_ref, ln_b_ref, wpw_ref, bpw_ref,
                       *rest, n_lent):
    o_ref, s_ref, ext, shifted, ybuf = rest[n_lent:]
    t = pl.program_id(1)
    nt = pl.num_programs(1)
    tt = CONV_TT
    sub = 8

    @pl.when(t == 0)
    def _():
        ext[0:HIST, :] = jnp.zeros((HIST, CONV_W), F32)

    a = z7_ref[:, 0:CONV_W]
    bgate = jnp.concatenate([z7_ref[:, CONV_W:_CBLK], z8_ref[:, 0:2 * CONV_W - _CBLK]], axis=-1)
    g = z8_ref[:, 2 * CONV_W - _CBLK:]
    ext[HIST:HIST + tt, :] = a * _sigmoid(bgate)

    base = HIST - (CONV_K - 1)
    n_sh = HIST + tt - sub
    for r in range(1, sub):
        shifted[r, 0:n_sh, :] = ext[r:r + n_sh, :]
    for ci in range(tt // CONV_RC):
        r0 = ci * CONV_RC
        acc = jnp.zeros((CONV_RC // sub, sub, CONV_W), F32)
        for k in range(CONV_K):
            r = (base + k) % sub
            a0 = r0 + (base + k) - r
            rows = ext[a0:a0 + CONV_RC, :] if r == 0 else shifted[r, a0:a0 + CONV_RC, :]
            acc = acc + wdw_ref[k][None] * rows.reshape(CONV_RC // sub, sub, CONV_W)
        ybuf[r0:r0 + CONV_RC, :] = acc.reshape(CONV_RC, CONV_W)
    y = ybuf[...] + bdw_ref[...]
    o_ref[...] = _conv_tail(y, g, ln_g_ref, ln_b_ref, wpw_ref, bpw_ref).astype(o_ref.dtype)

    @pl.when(t == nt - 1)
    def _():
        s_ref[0] = ext[tt + HIST - (CONV_K - 1):tt + HIST, :]

    ext[0:HIST, :] = ext[tt:tt + HIST, :]


def _conv_prefill(z, mix, w_dw, b_dw, ln_g, ln_b, wpw_all, b_pw, batch, seq, layer, s_all):
    tt = CONV_TT
    nt = seq // tt
    vec = lambda i, t: (0, 0)
    in_specs = [pl.BlockSpec((tt, _CBLK), lambda b, t: (b * nt + t, OFF_AC // _CBLK)),
                pl.BlockSpec((tt, _CBLK), lambda b, t: (b * nt + t, OFF_AC // _CBLK + 1)),
                pl.BlockSpec((CONV_K, 8, CONV_W), lambda b, t: (0, 0, 0)),
                pl.BlockSpec((1, CONV_W), vec),
                pl.BlockSpec((1, CONV_W), vec),
                pl.BlockSpec((1, CONV_W), vec),
                pl.BlockSpec((None, CONV_W, CONV_W), lambda b, t: (layer, 0, 0)),
                pl.BlockSpec((1, CONV_W), vec)]
    lent_specs, aliases, lent = _lend(len(in_specs), [mix, s_all], [0, 1])
    return pl.pallas_call(
        functools.partial(_conv_prefill_body, n_lent=len(lent)),
        grid=(batch, nt),
        in_specs=in_specs + lent_specs,
        out_specs=[pl.BlockSpec((tt, CONV_W), lambda b, t: (b * nt + t, (RET_W + HG_W) // CONV_W)),
                   pl.BlockSpec((None, 1, CONV_K - 1, CONV_W), lambda b, t: (layer, b, 0, 0))],
        out_shape=[jax.ShapeDtypeStruct(mix.shape, BF16),
                   jax.ShapeDtypeStruct((DEPTH, batch, CONV_K - 1, CONV_W), F32)],
        scratch_shapes=[pltpu.VMEM((HIST + tt, CONV_W), F32), pltpu.VMEM((8, HIST + tt, CONV_W), F32),
                        pltpu.VMEM((tt, CONV_W), F32)],
        input_output_aliases=aliases,
        compiler_params=_params(("parallel", "arbitrary")),
        name="conv_prefill",
    )(z, z, jnp.broadcast_to(w_dw[:, None, :], (CONV_K, 8, CONV_W)), b_dw.reshape(1, -1),
      ln_g.reshape(1, -1), ln_b.reshape(1, -1), wpw_all, b_pw.reshape(1, -1), *lent)


def _ret_decode_body(qT_ref, kT_ref, v_ref, g_ref, cos_ref, sin_ref, lg_ref, gn_ref, s_ref, *rest,
                     n_lent):
    o_ref, so_ref, orow = rest[n_lent:]
    half = RET_DK // 2
    cosc = cos_ref[...]
    sinc = sin_ref[...]
    for h in range(RET_HEADS):
        zq = qT_ref[h]
        zk = kT_ref[h]
        q = zq * cosc + jnp.concatenate([zq[half:], zq[:half]], axis=0) * sinc
        k = (zk * cosc + jnp.concatenate([zk[half:], zk[:half]], axis=0) * sinc) * (RET_DK ** -0.5)
        gamma = jnp.exp(lg_ref[h:h + 1, :])
        for j in range(DEC_BB):
            vrow = v_ref[j:j + 1, h * RET_DV:(h + 1) * RET_DV]
            s1 = gamma * s_ref[j, h] + k[:, j:j + 1] * vrow
            so_ref[j, h] = s1
            orow[j:j + 1, h * RET_DV:(h + 1) * RET_DV] = jnp.sum(q[:, j:j + 1] * s1, axis=0, keepdims=True)
    for h in range(RET_HEADS):
        sl = slice(h * RET_DV, (h + 1) * RET_DV)
        o = orow[:, sl]
        mu = jnp.mean(o, axis=-1, keepdims=True)
        var = jnp.mean(jnp.square(o - mu), axis=-1, keepdims=True)
        on = (o - mu) * lax.rsqrt(var + EPS) * gn_ref[:, sl]
        o_ref[:, sl] = on * _silu(g_ref[:, sl])


def _cols(zs, off, heads, dk):
    bsz = zs.shape[0]
    x = zs[:, off:off + heads * dk].reshape(bsz // DEC_BB, DEC_BB, heads, dk)
    return x.transpose(0, 2, 3, 1)


def _ret_decode(zs, state_all, layer, cosc, sinc, lg_t, gn_g, so_all):
    bsz = zs.shape[0]
    bb = DEC_BB
    qT = _cols(zs, OFF_QR, RET_HEADS, RET_DK)
    kT = _cols(zs, OFF_KR, RET_HEADS, RET_DK)
    col_spec = pl.BlockSpec((None, RET_HEADS, RET_DK, bb), lambda i: (i, 0, 0, 0))
    st_spec = pl.BlockSpec((None, bb, RET_HEADS, RET_DK, RET_DV), lambda i: (layer, i, 0, 0, 0))
    in_specs = [col_spec, col_spec,
                pl.BlockSpec((bb, RET_W), lambda i: (i, OFF_VR // RET_W)),
                pl.BlockSpec((bb, RET_W), lambda i: (i, OFF_GR // RET_W)),
                pl.BlockSpec((RET_DK, bb), lambda i: (0, 0)),
                pl.BlockSpec((RET_DK, bb), lambda i: (0, 0)),
                pl.BlockSpec((RET_HEADS, RET_DV), lambda i: (0, 0)),
                pl.BlockSpec((1, RET_W), lambda i: (0, 0)),
                st_spec]
    lent_specs, aliases, lent = _lend(len(in_specs), [so_all], [1])
    return pl.pallas_call(
        functools.partial(_ret_decode_body, n_lent=len(lent)),
        grid=(bsz // bb,),
        in_specs=in_specs + lent_specs,
        out_specs=[pl.BlockSpec((bb, RET_W), lambda i: (i, 0)), st_spec],
        out_shape=[jax.ShapeDtypeStruct((bsz, RET_W), F32),
                   jax.ShapeDtypeStruct(state_all.shape, F32)],
        scratch_shapes=[pltpu.VMEM((bb, RET_W), F32)],
        input_output_aliases=aliases,
        compiler_params=_params(("parallel",)),
        name="ret_decode",
    )(qT, kT, zs, zs, cosc, sinc, lg_t, gn_g.reshape(1, RET_W), state_all, *lent)


def _hgrn_decode_body(q_ref, fT_ref, v_ref, g_ref, lbT_ref, ng_ref, s_ref, *rest, n_lent):
    o_ref, so_ref, orow = rest[n_lent:]
    for h in range(HG_HEADS):
        sl = slice(h * HG_DV, (h + 1) * HG_DV)
        qb = _silu(q_ref[:, sl]).astype(BF16)
        lb = lbT_ref[h]
        f = lb + (1.0 - lb) * _sigmoid(fT_ref[h])
        for j in range(DEC_BB):
            vrow = v_ref[j:j + 1, sl]
            s1 = f[:, j:j + 1] * (s_ref[j, h] - vrow) + vrow
            so_ref[j, h] = s1
            oj = jnp.dot(qb, s1.astype(BF16), preferred_element_type=F32)
            orow[j:j + 1, sl] = oj[j:j + 1, :]
    for h in range(HG_HEADS):
        sl = slice(h * HG_DV, (h + 1) * HG_DV)
        o = orow[:, sl]
        on = o * lax.rsqrt(jnp.mean(o * o, axis=-1, keepdims=True) + EPS) * ng_ref[:, sl]
        o_ref[:, sl] = on * _silu(g_ref[:, sl])


def _hgrn_decode(zs, state_all, layer, lb, ng, so_all):
    bsz = zs.shape[0]
    bb = DEC_BB
    fT = _cols(zs, OFF_FH, HG_HEADS, HG_DK)
    col_spec = pl.BlockSpec((None, HG_HEADS, HG_DK, bb), lambda i: (i, 0, 0, 0))
    st_spec = pl.BlockSpec((None, bb, HG_HEADS, HG_DK, HG_DV), lambda i: (layer, i, 0, 0, 0))
    in_specs = [pl.BlockSpec((bb, HG_W), lambda i: (i, OFF_QH // HG_W)),
                col_spec,
                pl.BlockSpec((bb, HG_W), lambda i: (i, OFF_IH // HG_W)),
                pl.BlockSpec((bb, HG_W), lambda i: (i, OFF_GH // HG_W)),
                pl.BlockSpec((HG_HEADS, HG_DK, 1), lambda i: (0, 0, 0)),
                pl.BlockSpec((1, HG_W), lambda i: (0, 0)),
                st_spec]
    lent_specs, aliases, lent = _lend(len(in_specs), [so_all], [1])
    return pl.pallas_call(
        functools.partial(_hgrn_decode_body, n_lent=len(lent)),
        grid=(bsz // bb,),
        in_specs=in_specs + lent_specs,
        out_specs=[pl.BlockSpec((bb, HG_W), lambda i: (i, 0)), st_spec],
        out_shape=[jax.ShapeDtypeStruct((bsz, HG_W), F32),
                   jax.ShapeDtypeStruct(state_all.shape, F32)],
        scratch_shapes=[pltpu.VMEM((bb, HG_W), F32)],
        input_output_aliases=aliases,
        compiler_params=_params(("parallel",)),
        name="hgrn_decode",
    )(zs, fT, zs, zs, lb.reshape(HG_HEADS, HG_DK, 1), ng.reshape(1, HG_W), state_all, *lent)


def _conv_decode_body(z7_ref, z8_ref, wdw_ref, bdw_ref, ln_g_ref, ln_b_ref, wpw_ref, bpw_ref, s_ref,
                      *rest, n_lent):
    o_ref, so_ref = rest[n_lent:]
    kh = CONV_K - 1
    a = z7_ref[:, 0:CONV_W]
    bgate = jnp.concatenate([z7_ref[:, CONV_W:_CBLK], z8_ref[:, 0:2 * CONV_W - _CBLK]], axis=-1)
    g = z8_ref[:, 2 * CONV_W - _CBLK:]
    u = a * _sigmoid(bgate)
    buf = s_ref[...]
    y = jnp.sum(buf * wdw_ref[0:kh, :][None], axis=1) + u * wdw_ref[kh:kh + 1, :] + bdw_ref[...]
    o_ref[...] = _conv_tail(y, g, ln_g_ref, ln_b_ref, wpw_ref, bpw_ref)
    so_ref[:, 0:kh - 1, :] = s_ref[:, 1:kh, :]
    for j in range(DEC_BB):
        so_ref[j, kh - 1:kh, :] = u[j:j + 1, :]


def _conv_decode(zs, state_all, layer, w_dw, b_dw, ln_g, ln_b, wpw_all, b_pw, so_all):
    bsz = zs.shape[0]
    bb = DEC_BB
    vec = lambda i: (0, 0)
    st_spec = pl.BlockSpec((None, bb, CONV_K - 1, CONV_W), lambda i: (layer, i, 0, 0))
    in_specs = [pl.BlockSpec((bb, _CBLK), lambda i: (i, OFF_AC // _CBLK)),
                pl.BlockSpec((bb, _CBLK), lambda i: (i, OFF_AC // _CBLK + 1)),
                pl.BlockSpec((CONV_K, CONV_W), vec),
                pl.BlockSpec((1, CONV_W), vec),
                pl.BlockSpec((1, CONV_W), vec),
                pl.BlockSpec((1, CONV_W), vec),
                pl.BlockSpec((None, CONV_W, CONV_W), lambda i: (layer, 0, 0)),
                pl.BlockSpec((1, CONV_W), vec),
                st_spec]
    lent_specs, aliases, lent = _lend(len(in_specs), [so_all], [1])
    return pl.pallas_call(
        functools.partial(_conv_decode_body, n_lent=len(lent)),
        grid=(bsz // bb,),
        in_specs=in_specs + lent_specs,
        out_specs=[pl.BlockSpec((bb, CONV_W), lambda i: (i, 0)), st_spec],
        out_shape=[jax.ShapeDtypeStruct((bsz, CONV_W), F32),
                   jax.ShapeDtypeStruct(state_all.shape, F32)],
        input_output_aliases=aliases,
        compiler_params=_params(("parallel",)),
        name="conv_decode",
    )(zs, zs, w_dw, b_dw.reshape(1, -1), ln_g.reshape(1, -1), ln_b.reshape(1, -1), wpw_all,
      b_pw.reshape(1, -1), state_all, *lent)


_N_DEC_IN = 16


def _decode_one_sequence(ins, outs, scr):
    (rq_ref, rk_ref, rv_ref, rg_ref, cos_ref, sin_ref, lg_ref, gn_ref, rs_ref,
     hq_ref, hf_ref, hv_ref, hg_ref, lbT_ref, ng_ref, hs_ref) = ins
    ro_ref, rso_ref, ho_ref, hso_ref = outs
    rrow, hrow, tsc = scr
    half = RET_DK // 2
    cosc = cos_ref[...]
    sinc = sin_ref[...]
    for h in range(RET_HEADS):
        tsc[h:h + 1, :] = rq_ref[:, h * RET_DK:(h + 1) * RET_DK]
        tsc[RET_HEADS + h:RET_HEADS + h + 1, :] = rk_ref[:, h * RET_DK:(h + 1) * RET_DK]
    for h in range(HG_HEADS):
        tsc[2 * RET_HEADS + h:2 * RET_HEADS + h + 1, :] = hf_ref[:, h * HG_DK:(h + 1) * HG_DK]
    cols = tsc[...].T
    for h in range(RET_HEADS):
        sl = slice(h * RET_DV, (h + 1) * RET_DV)
        zq = cols[:, h:h + 1]
        zk = cols[:, RET_HEADS + h:RET_HEADS + h + 1]
        q = zq * cosc + jnp.concatenate([zq[half:], zq[:half]], axis=0) * sinc
        k = (zk * cosc + jnp.concatenate([zk[half:], zk[:half]], axis=0) * sinc) * (RET_DK ** -0.5)
        s1 = jnp.exp(lg_ref[h:h + 1, :]) * rs_ref[h] + k * rv_ref[:, sl]
        rso_ref[h] = s1
        rrow[0:1, sl] = jnp.sum(q * s1, axis=0, keepdims=True)
    for h in range(RET_HEADS):
        sl = slice(h * RET_DV, (h + 1) * RET_DV)
        o = rrow[0:1, sl]
        mu = jnp.mean(o, axis=-1, keepdims=True)
        var = jnp.mean(jnp.square(o - mu), axis=-1, keepdims=True)
        on = (o - mu) * lax.rsqrt(var + EPS) * gn_ref[:, sl]
        ro_ref[:, sl] = on * _silu(rg_ref[:, sl])
    for h in range(HG_HEADS):
        sl = slice(h * HG_DV, (h + 1) * HG_DV)
        qb = jnp.broadcast_to(_silu(hq_ref[:, sl]), (8, HG_DK)).astype(BF16)
        lb = lbT_ref[h]
        zf = cols[:, 2 * RET_HEADS + h:2 * RET_HEADS + h + 1]
        f = lb + (1.0 - lb) * _sigmoid(zf)
        vrow = hv_ref[:, sl]
        s1 = f * (hs_ref[h] - vrow) + vrow
        hso_ref[h] = s1
        hrow[0:1, sl] = jnp.dot(qb, s1.astype(BF16), preferred_element_type=F32)[0:1, :]
    for h in range(HG_HEADS):
        sl = slice(h * HG_DV, (h + 1) * HG_DV)
        o = hrow[0:1, sl]
        on = o * lax.rsqrt(jnp.mean(o * o, axis=-1, keepdims=True) + EPS) * ng_ref[:, sl]
        ho_ref[:, sl] = on * _silu(hg_ref[:, sl])


def _inproj_decode_body(x_ref, w_ref, *rest, n_lent, cast_next):
    dec_in = rest[:_N_DEC_IN]
    rest = rest[_N_DEC_IN:]
    if cast_next:
        nxt_ref = rest[0]
        z_ref, ro_ref, rso_ref, ho_ref, hso_ref, nxtb_ref, acc_ref, rrow, hrow, tsc = rest[1 + n_lent:]
    else:
        z_ref, ro_ref, rso_ref, ho_ref, hso_ref, acc_ref, rrow, hrow, tsc = rest[n_lent:]
    k = pl.program_id(2)

    def side_work():
        _decode_one_sequence(dec_in, (ro_ref, rso_ref, ho_ref, hso_ref), (rrow, hrow, tsc))
        if cast_next:
            nxtb_ref[...] = nxt_ref[...].astype(BF16)

    first = jnp.logical_and(jnp.logical_and(pl.program_id(0) == 0, pl.program_id(1) == 0), k == 0)

    @pl.when(first)
    def _():
        tsc[...] = jnp.zeros_like(tsc)

    @pl.when(k == 0)
    def _():
        acc_ref[...] = jnp.dot(x_ref[...], w_ref[...], preferred_element_type=F32)
        side_work()

    @pl.when(k == 1)
    def _():
        z_ref[...] = acc_ref[...] + jnp.dot(x_ref[...], w_ref[...], preferred_element_type=F32)
        side_work()


def _inproj_with_decode(x, w_b, zs, state_ret, state_hgrn, layer, cos1, sin1, lg_t, gn_g, lb, ng,
                        rso_all, hso_all, *, tm, tn, tk, next_w=None):
    m, kdim = x.shape
    n = w_b.shape[-1]
    bsz = zs.shape[0]
    nj, nk = n // tn, kdim // tk
    assert nk == 2 and (m // tm) * nj * nk >= bsz
    seq = lambda i, j, k: jnp.minimum((i * nj + j) * nk + k, bsz - 1)
    z3 = zs.reshape(bsz, 1, IN_W)

    def row_spec(width, off):
        return pl.BlockSpec((None, 1, width), lambda i, j, k: (seq(i, j, k), 0, off // width))

    def state_spec(heads, dk, dv):
        return pl.BlockSpec((None, None, heads, dk, dv), lambda i, j, k: (layer, seq(i, j, k), 0, 0, 0))

    const2 = lambda i, j, k: (0, 0)
    in_specs = [pl.BlockSpec((tm, tk), lambda i, j, k: (i, k)),
                pl.BlockSpec((tk, tn), lambda i, j, k: (k, j)),
                row_spec(RET_HEADS * RET_DK, OFF_QR), row_spec(RET_HEADS * RET_DK, OFF_KR),
                row_spec(RET_W, OFF_VR), row_spec(RET_W, OFF_GR),
                pl.BlockSpec((RET_DK, 1), const2), pl.BlockSpec((RET_DK, 1), const2),
                pl.BlockSpec((RET_HEADS, RET_DV), const2), pl.BlockSpec((1, RET_W), const2),
                state_spec(RET_HEADS, RET_DK, RET_DV),
                row_spec(HG_W, OFF_QH), row_spec(HG_W, OFF_FH),
                row_spec(HG_W, OFF_IH), row_spec(HG_W, OFF_GH),
                pl.BlockSpec((HG_HEADS, HG_DK, 1), lambda i, j, k: (0, 0, 0)),
                pl.BlockSpec((1, HG_W), const2),
                state_spec(HG_HEADS, HG_DK, HG_DV)]
    args = [x, w_b,
            z3, z3, z3, z3,
            cos1, sin1, lg_t, gn_g.reshape(1, RET_W), state_ret,
            z3, z3, z3, z3,
            lb.reshape(HG_HEADS, HG_DK, 1), ng.reshape(1, HG_W), state_hgrn]
    assert len(in_specs) == 2 + _N_DEC_IN
    out_specs = [pl.BlockSpec((tm, tn), lambda i, j, k: (i, j)),
                 pl.BlockSpec((None, 1, RET_W), lambda i, j, k: (seq(i, j, k), 0, 0)),
                 state_spec(RET_HEADS, RET_DK, RET_DV),
                 pl.BlockSpec((None, 1, HG_W), lambda i, j, k: (seq(i, j, k), 0, 0)),
                 state_spec(HG_HEADS, HG_DK, HG_DV)]
    out_shape = [jax.ShapeDtypeStruct((m, n), F32),
                 jax.ShapeDtypeStruct((bsz, 1, RET_W), F32),
                 jax.ShapeDtypeStruct(state_ret.shape, F32),
                 jax.ShapeDtypeStruct((bsz, 1, HG_W), F32),
                 jax.ShapeDtypeStruct(state_hgrn.shape, F32)]
    if next_w is not None:
        slab = next_w.shape[1] // bsz
        assert slab * bsz == next_w.shape[1] and slab % 16 == 0
        in_specs.append(pl.BlockSpec((None, slab, n), lambda i, j, k: (layer + 1, seq(i, j, k), 0)))
        args.append(next_w)
        out_specs.append(pl.BlockSpec((slab, n), lambda i, j, k: (seq(i, j, k), 0)))
        out_shape.append(jax.ShapeDtypeStruct(next_w.shape[1:], BF16))
    lent_specs, aliases, lent = _lend(len(in_specs), [rso_all, hso_all], [2, 4])
    outs = pl.pallas_call(
        functools.partial(_inproj_decode_body, n_lent=len(lent), cast_next=next_w is not None),
        grid=(m // tm, nj, nk),
        in_specs=in_specs + lent_specs,
        out_specs=out_specs,
        out_shape=out_shape,
        scratch_shapes=[pltpu.VMEM((tm, tn), F32), pltpu.VMEM((8, RET_W), F32), pltpu.VMEM((8, HG_W), F32),
                        pltpu.VMEM((RET_DK, RET_DK), F32)],
        input_output_aliases=aliases,
        compiler_params=_params(("arbitrary", "arbitrary", "arbitrary")),
        name="inproj_decode",
    )(*args, *lent)
    z, ro, rso, ho, hso = outs[:5]
    return (z, ro.reshape(bsz, RET_W), rso, ho.reshape(bsz, HG_W), hso) + tuple(outs[5:])


def _rope_tables(pos):
    half = RET_DK // 2
    inv = ROPE_BASE ** (-jnp.arange(half, dtype=F32) / half)
    ang = pos[:, None] * inv[None, :]
    cos = jnp.cos(ang)
    sin = jnp.sin(ang)
    return jnp.concatenate([cos, cos], axis=-1), jnp.concatenate([-sin, sin], axis=-1)


PROMPT_TILES_IN = (1024, 1536, 2048)
PROMPT_TILES_SQ = (1024, 512, D_MODEL)
SAMPLE_TN_IN = 768
SAMPLE_TN_SQ = 512


def kernel(x_prompt, x_sample, state_ret, state_hgrn, state_conv, p_prompt, p_sample, norm_g, w_in,
           ret_gn_g, hg_lower_bounds, hg_norm_g, w_dw, b_dw, conv_ln_g, conv_ln_b, w_pw, b_pw, w_out,
           w_ple, w_pg, final_norm_g):
    bp, tp, d = x_prompt.shape
    bs, ts, _ = x_sample.shape
    mp, ms = bp * tp, bs * ts
    assert ts == 1 and tp % CONV_TT == 0 and bs % DEC_BB == 0

    w_ple_b = w_ple.astype(BF16)
    w_pw_b = w_pw.astype(BF16)
    pp_b = p_prompt.reshape(DEPTH, mp, PLE_DIM).astype(BF16)
    ps_b = p_sample.reshape(DEPTH, ms, PLE_DIM).astype(BF16)

    lbs = jax.nn.softmax(hg_lower_bounds.astype(F32), axis=0)
    lbs = jnp.cumsum(lbs, axis=0) - lbs[0:1]
    log_gamma = np.log(1.0 - 2.0 ** (-5.0 - np.arange(RET_HEADS, dtype=np.float32))).astype(np.float32)
    lg_t = jnp.asarray(np.broadcast_to(log_gamma[:, None], (RET_HEADS, RET_DV)))

    cos_p, sin_p = _rope_tables(jnp.arange(tp, dtype=F32))
    cos_s, sin_s = _rope_tables(PAST_LEN + jnp.arange(ts, dtype=F32))
    cosc = jnp.broadcast_to(cos_s.reshape(RET_DK, 1), (RET_DK, DEC_BB))
    sinc = jnp.broadcast_to(sin_s.reshape(RET_DK, 1), (RET_DK, DEC_BB))

    cos1, sin1 = cos_s.reshape(RET_DK, 1), sin_s.reshape(RET_DK, 1)

    def layer(hp, hs, l, st_p, st_s, weights_b):
        pr, ph, pc = st_p
        sr, sh, sc = st_s
        w_in_b, w_out_b, w_pg_b = weights_b
        nxt = l + 1 < DEPTH
        xs = _rmsnorm(hs, norm_g[l], BF16, ms)
        if w_in_b is None:
            zs, w_in_b = _mm(xs, w_in, l, mode="plain", tm=ms, tn=SAMPLE_TN_IN, tk=d)
        else:
            (zs,) = _mm(xs, w_in_b, l, mode="plain", tm=ms, tn=SAMPLE_TN_IN, tk=d)

        tm, tn, tk = PROMPT_TILES_IN
        xp = _rmsnorm(hp, norm_g[l], BF16, 256)
        zp, mix_r, sr, mix_h, sh = _inproj_with_decode(
            xp, w_in_b, zs, state_ret, state_hgrn, l, cos1, sin1, lg_t, ret_gn_g[l], lbs[l],
            hg_norm_g[l], sr, sh, tm=tm, tn=tn, tk=tk)

        mix_c, sc = _conv_decode(zs, state_conv, l, w_dw[l], b_dw[l], conv_ln_g[l], conv_ln_b[l],
                                 w_pw_b, b_pw[l], sc)
        mix_s = jnp.concatenate([mix_r, mix_h, mix_c], axis=-1).astype(BF16)
        if w_out_b is None:
            h1, h1b, w_out_b = _mm(mix_s, w_out, l, mode="resid", tm=ms, tn=SAMPLE_TN_SQ, tk=d, resid=hs)
            hs, w_pg_b = _mm(h1b, w_pg, l, mode="ple", tm=ms, tn=SAMPLE_TN_SQ, tk=d, resid=h1, p=ps_b,
                             wp_all=w_ple_b)
        else:
            h1, h1b = _mm(mix_s, w_out_b, l, mode="resid", tm=ms, tn=SAMPLE_TN_SQ, tk=d, resid=hs)
            (hs,) = _mm(h1b, w_pg_b, l, mode="ple", tm=ms, tn=SAMPLE_TN_SQ, tk=d, resid=h1, p=ps_b,
                        wp_all=w_ple_b)

        mix, pr = _ret_prefill(zp, cos_p, sin_p, lg_t, ret_gn_g[l], bp, tp, l, pr)
        mix, ph, *w_in_next = _hgrn_prefill(zp, mix, lbs[l], hg_norm_g[l], bp, tp, l, ph,
                                            next_w=w_in if nxt else None)
        mix, pc = _conv_prefill(zp, mix, w_dw[l], b_dw[l], conv_ln_g[l], conv_ln_b[l], w_pw_b,
                                b_pw[l], bp, tp, l, pc)
        tm, tn, tk = PROMPT_TILES_SQ
        h1, h1b, *w_out_next = _mm(mix, w_out_b, l, mode="resid", tm=tm, tn=tn, tk=tk, resid=hp,
                                   next_w=w_out if nxt else None)
        hp, *w_pg_next = _mm(h1b, w_pg_b, l, mode="ple", tm=tm, tn=tn, tk=tk, resid=h1, p=pp_b,
                             wp_all=w_ple_b, next_w=w_pg if nxt else None)
        weights_next = (w_in_next[0], w_out_next[0], w_pg_next[0]) if nxt else None
        return hp, hs, (pr, ph, pc), (sr, sh, sc), weights_next

    hp, hs = x_prompt.reshape(mp, d), x_sample.reshape(ms, d)
    st_p = st_s = (None, None, None)
    weights_b = (None, None, None)
    for l in range(DEPTH):
        hp, hs, st_p, st_s, weights_b = layer(hp, hs, l, st_p, st_s, weights_b)
    y_p = _rmsnorm(hp, final_norm_g, F32, 256).reshape(x_prompt.shape)
    y_s = _rmsnorm(hs, final_norm_g, F32, ms).reshape(x_sample.shape)
    return (y_p, y_s) + st_p + st_s
```
